```python
import math
import jax, jax.numpy as jnp
from jax import lax
import numpy as np

D_MODEL = 2048
BATCH = 4
SEQ = 4096
DEPTH = 4
DEC_BATCH = 1
DEC_SEQ = 8192
PAST_LEN = 128

N_HEADS = 8
HEAD_DIM = 128
ATTN_WIDTH = N_HEADS * HEAD_DIM
CONV_CH = D_MODEL - ATTN_WIDTH
CONV_WIDTH = 31
DILATED_CONFIGS = ((128, 1), (512, 4), (2048, 16))
N_GROUPS = 4
EXPERTS_PER_GROUP = 8
N_EXPERTS = N_GROUPS * EXPERTS_PER_GROUP
TOP_K = 2
D_EXPERT = 1024
MOE_BLOCK = 128
IN_WIDTH = 3 * ATTN_WIDTH + 2 * CONV_CH
NORM_EPS = 1e-6
MASK_VALUE = -1e30

kernel_name = "hybrid_dilated_attn_conformer_conv_hmoe_encoder"


def rms_norm(x, g):
    xf = x.astype(jnp.float32)
    y = xf * lax.rsqrt(jnp.mean(xf * xf, axis=-1, keepdims=True) + NORM_EPS)
    return (y * g.astype(jnp.float32)).astype(x.dtype)


def layer_norm(x, g, b):
    xf = x.astype(jnp.float32)
    mu = jnp.mean(xf, axis=-1, keepdims=True)
    var = jnp.mean(jnp.square(xf - mu), axis=-1, keepdims=True)
    y = (xf - mu) * lax.rsqrt(var + NORM_EPS)
    return (y * g.astype(jnp.float32) + b.astype(jnp.float32)).astype(x.dtype)


def alibi_slopes():
    return jnp.exp2(-8.0 * jnp.arange(1, N_HEADS + 1, dtype=jnp.float32) / N_HEADS)


def dilated_window_attention(q, k, v, window, dilation):
    B, S, H, E = q.shape
    W = window // (2 * dilation)
    L = S // dilation
    nb = -(-L // W)
    Lp = nb * W

    def to_blocks(t):
        t = t.reshape(B, L, dilation, H, E).transpose(0, 2, 1, 3, 4)
        t = jnp.pad(t, ((0, 0), (0, 0), (0, Lp - L), (0, 0), (0, 0)))
        return t.reshape(B, dilation, nb, W, H, E)

    def neighbours(t):
        tp = jnp.pad(t, ((0, 0), (0, 0), (1, 1), (0, 0), (0, 0), (0, 0)))
        return jnp.concatenate([tp[:, :, :-2], tp[:, :, 1:-1], tp[:, :, 2:]], axis=3)

    qb = to_blocks(q)
    kb = neighbours(to_blocks(k))
    vb = neighbours(to_blocks(v)).astype(jnp.float32)
    s = jnp.einsum('brnqhe,brnkhe->brnhqk', qb, kb,
                   preferred_element_type=jnp.float32) * (E ** -0.5)

    a = jnp.arange(W)
    c = jnp.arange(3 * W)
    rel = c[None, :] - W - a[:, None]
    key_idx = jnp.arange(nb)[:, None] * W - W + c[None, :]
    valid = ((jnp.abs(rel)[None] <= W)
             & (key_idx[:, None, :] >= 0) & (key_idx[:, None, :] < L))
    dist = (dilation * jnp.abs(rel)).astype(jnp.float32)
    bias = -alibi_slopes()[:, None, None] * dist[None]
    s = jnp.where(valid[:, None], s + bias, MASK_VALUE)

    m = jnp.max(s, axis=-1, keepdims=True)
    p = jnp.exp(s - m)
    l = jnp.sum(p, axis=-1, keepdims=True)
    o = jnp.einsum('brnhqk,brnkhe->brnqhe', p, vb) / jnp.swapaxes(l, 3, 4)
    lse = jnp.swapaxes((m + jnp.log(l))[..., 0], 3, 4)

    def from_blocks(t):
        rest = t.shape[4:]
        t = t.reshape(B, dilation, Lp, *rest)[:, :, :L]
        t = jnp.swapaxes(t, 1, 2)
        return t.reshape(B, S, *rest)

    return from_blocks(o), from_blocks(lse)


def dilated_mixture_attention(q, k, v):
    outs = []
    lses = []
    for window, dilation in DILATED_CONFIGS:
        o, lse = dilated_window_attention(q, k, v, window, dilation)
        outs.append(o)
        lses.append(lse)
    wts = jax.nn.softmax(jnp.stack(lses), axis=0)
    return jnp.sum(wts[..., None] * jnp.stack(outs), axis=0)


def conformer_conv(ga, gb, conv_w, conv_b, ln_g, ln_b):
    u = ga * jax.nn.sigmoid(gb)
    u = lax.conv_general_dilated(
        u, conv_w[:, None, :].astype(u.dtype), window_strides=(1,),
        padding=[(CONV_WIDTH // 2, CONV_WIDTH // 2)],
        dimension_numbers=('NWC', 'WIO', 'NWC'),
        feature_group_count=CONV_CH) + conv_b
    u = layer_norm(u, ln_g, ln_b)
    return jax.nn.silu(u)


def sparse_expert_mix(h, experts, gates, e_gate, e_up, e_down):
    T, D = h.shape
    A = T * TOP_K
    NB = -(-(A + N_EXPERTS * (MOE_BLOCK - 1)) // MOE_BLOCK)
    flat_e = experts.reshape(-1)
    flat_tok = jnp.repeat(jnp.arange(T, dtype=jnp.int32), TOP_K)
    flat_w = gates.reshape(-1)
    order = jnp.argsort(flat_e)
    sorted_e = flat_e[order]
    counts = jnp.bincount(flat_e, length=N_EXPERTS)
    starts = jnp.cumsum(counts) - counts
    padded = ((counts + MOE_BLOCK - 1) // MOE_BLOCK) * MOE_BLOCK
    padded_ends = jnp.cumsum(padded)
    padded_starts = padded_ends - padded
    dest = padded_starts[sorted_e] + (jnp.arange(A) - starts[sorted_e])
    row_tok = jnp.full((NB * MOE_BLOCK,), T, jnp.int32).at[dest].set(flat_tok[order])
    row_w = jnp.zeros((NB * MOE_BLOCK,), h.dtype).at[dest].set(flat_w[order])
    block_expert = jnp.minimum(
        jnp.searchsorted(padded_ends, jnp.arange(NB) * MOE_BLOCK, side='right'), N_EXPERTS - 1)
    h_pad = jnp.concatenate([h, jnp.zeros((1, D), h.dtype)], axis=0)
    xs = h_pad[row_tok].reshape(NB, MOE_BLOCK, D)

    def expert_block(args):
        xb, e = args
        return (jax.nn.silu(xb @ e_gate[e]) * (xb @ e_up[e])) @ e_down[e]

    ys = lax.map(expert_block, (xs, block_expert)).reshape(NB * MOE_BLOCK, D)
    out = jnp.zeros((T + 1, D), ys.dtype).at[row_tok].add(ys * row_w[:, None])
    return out[:T]


def hierarchical_moe(h, rg_w, rg_b, re_w, re_b, e_gate, e_up, e_down):
    T = h.shape[0]
    lg = jnp.dot(h, rg_w, preferred_element_type=jnp.float32) + rg_b.astype(jnp.float32)
    pg = jax.nn.softmax(lg, axis=-1)
    g_idx = jnp.argmax(lg, axis=-1).astype(jnp.int32)
    pg_sel = jnp.take_along_axis(pg, g_idx[:, None], axis=1)
    le = (jnp.dot(h, re_w, preferred_element_type=jnp.float32)
          + re_b.astype(jnp.float32)).reshape(T, N_GROUPS, EXPERTS_PER_GROUP)
    le_sel = jnp.take_along_axis(le, g_idx[:, None, None], axis=1)[:, 0]
    pe = jax.nn.softmax(le_sel, axis=-1)
    top_p, top_i = lax.top_k(pe, TOP_K)
    gates = pg_sel * top_p / jnp.sum(top_p, axis=-1, keepdims=True)
    experts = g_idx[:, None] * EXPERTS_PER_GROUP + top_i.astype(jnp.int32)
    return sparse_expert_mix(h, experts, gates.astype(h.dtype), e_gate, e_up, e_down)


def encoder_layer(x, mod, norm1_g, w_in, conv_w, conv_b, conv_ln_g, conv_ln_b, attn_out_g,
                  conv_out_g, w_out, norm2_g, rg_w, rg_b, re_w, re_b, e_gate, e_up, e_down):
    B, S, D = x.shape
    shift1, scale1, gate1, shift2, scale2, gate2 = [t[:, None, :] for t in jnp.split(mod, 6, axis=-1)]
    h = rms_norm(x, norm1_g) * (1 + scale1) + shift1
    proj = h @ w_in
    q, k, v, ga, gb = jnp.split(
        proj, [ATTN_WIDTH, 2 * ATTN_WIDTH, 3 * ATTN_WIDTH, 3 * ATTN_WIDTH + CONV_CH], axis=-1)
    hs = (B, S, N_HEADS, HEAD_DIM)
    attn = dilated_mixture_attention(q.reshape(hs), k.reshape(hs), v.reshape(hs))
    attn = attn.reshape(B, S, ATTN_WIDTH).astype(x.dtype)
    conv = conformer_conv(ga, gb, conv_w, conv_b, conv_ln_g, conv_ln_b)
    merged = jnp.concatenate([rms_norm(attn, attn_out_g), rms_norm(conv, conv_out_g)], axis=-1)
    x = x + gate1 * (merged @ w_out)
    h = rms_norm(x, norm2_g) * (1 + scale2) + shift2
    y = hierarchical_moe(h.reshape(B * S, D), rg_w, rg_b, re_w, re_b, e_gate, e_up, e_down)
    return x + gate2 * y.reshape(B, S, D)


def run_trunk(x, c, ada_w, ada_b, norm1_g, w_in, conv_w, conv_b, conv_ln_g, conv_ln_b,
              attn_out_g, conv_out_g, w_out, norm2_g, rg_w, rg_b, re_w, re_b,
              e_gate, e_up, e_down, final_g):
    sc = jax.nn.silu(c)
    for l in range(DEPTH):
        mod = sc @ ada_w[l] + ada_b[l]
        x = encoder_layer(x, mod, norm1_g[l], w_in[l], conv_w[l], conv_b[l], conv_ln_g[l],
                          conv_ln_b[l], attn_out_g[l], conv_out_g[l], w_out[l], norm2_g[l],
                          rg_w[l], rg_b[l], re_w[l], re_b[l], e_gate[l], e_up[l], e_down[l])
    return rms_norm(x, final_g)


def setup_inputs(seed: int = 0) -> dict:
    key = jax.random.key(seed)
    ks = jax.random.split(key, 26)
    f32 = jnp.float32

    def nrm(k, shape, scale):
        return jax.random.normal(k, shape, f32) * scale

    D = D_MODEL
    return {
        "x_prompt": nrm(ks[0], (BATCH, SEQ, D), 1.0),
        "x_sample": nrm(ks[1], (DEC_BATCH, DEC_SEQ, D), 1.0),
        "c_prompt": nrm(ks[2], (BATCH, D), 1.0),
        "c_sample": nrm(ks[3], (DEC_BATCH, D), 1.0),
        "ada_w": nrm(ks[4], (DEPTH, D, 6 * D), 0.5 * D ** -0.5),
        "ada_b": nrm(ks[5], (DEPTH, 6 * D), 0.02),
        "norm1_g": 1.0 + nrm(ks[6], (DEPTH, D), 0.02),
        "w_in": nrm(ks[7], (DEPTH, D, IN_WIDTH), D ** -0.5),
        "conv_w": nrm(ks[8], (DEPTH, CONV_WIDTH, CONV_CH), CONV_WIDTH ** -0.5),
        "conv_b": nrm(ks[9], (DEPTH, CONV_CH), 0.02),
        "conv_ln_g": 1.0 + nrm(ks[10], (DEPTH, CONV_CH), 0.02),
        "conv_ln_b": nrm(ks[11], (DEPTH, CONV_CH), 0.02),
        "attn_out_g": 1.0 + nrm(ks[12], (DEPTH, ATTN_WIDTH), 0.02),
        "conv_out_g": 1.0 + nrm(ks[13], (DEPTH, CONV_CH), 0.02),
        "w_out": nrm(ks[14], (DEPTH, D, D), D ** -0.5),
        "norm2_g": 1.0 + nrm(ks[15], (DEPTH, D), 0.02),
        "rg_w": nrm(ks[16], (DEPTH, D, N_GROUPS), D ** -0.5),
        "rg_b": nrm(ks[17], (DEPTH, N_GROUPS), 0.01),
        "re_w": nrm(ks[18], (DEPTH, D, N_EXPERTS), D ** -0.5),
        "re_b": nrm(ks[19], (DEPTH, N_EXPERTS), 0.01),
        "e_gate": nrm(ks[20], (DEPTH, N_EXPERTS, D, D_EXPERT), D ** -0.5),
        "e_up": nrm(ks[21], (DEPTH, N_EXPERTS, D, D_EXPERT), D ** -0.5),
        "e_down": nrm(ks[22], (DEPTH, N_EXPERTS, D_EXPERT, D), D_EXPERT ** -0.5),
        "final_g": 1.0 + nrm(ks[23], (D,), 0.02),
    }


def reference(x_prompt, x_sample, c_prompt, c_sample, ada_w, ada_b, norm1_g, w_in, conv_w, conv_b,
              conv_ln_g, conv_ln_b, attn_out_g, conv_out_g, w_out, norm2_g, rg_w, rg_b, re_w, re_b,
              e_gate, e_up, e_down, final_g):
    y_prompt = run_trunk(x_prompt, c_prompt, ada_w, ada_b, norm1_g, w_in, conv_w, conv_b, conv_ln_g,
                         conv_ln_b, attn_out_g, conv_out_g, w_out, norm2_g, rg_w, rg_b, re_w, re_b,
                         e_gate, e_up, e_down, final_g)
    y_sample = run_trunk(x_sample, c_sample, ada_w, ada_b, norm1_g, w_in, conv_w, conv_b, conv_ln_g,
                         conv_ln_b, attn_out_g, conv_out_g, w_out, norm2_g, rg_w, rg_b, re_w, re_b,
                         e_gate, e_up, e_down, final_g)
    return (y_prompt, y_sample)
```

```python
import functools
import math

import jax
import jax.numpy as jnp
from jax import lax
from jax.experimental import pallas as pl
from jax.experimental.pallas import tpu as pltpu

D_MODEL = 2048
DEPTH = 4
N_HEADS = 8
HEAD_DIM = 128
ATTN_WIDTH = N_HEADS * HEAD_DIM
CONV_CH = D_MODEL - ATTN_WIDTH
CONV_WIDTH = 31
DILATED_CONFIGS = ((128, 1), (512, 4), (2048, 16))
N_GROUPS = 4
EXPERTS_PER_GROUP = 8
N_EXPERTS = N_GROUPS * EXPERTS_PER_GROUP
TOP_K = 2
D_EXPERT = 1024
NORM_EPS = 1e-6
MASK_VALUE = -1e30

LANES = 128
ATT_TQ = 128
ATT_TK = 256
ATT_HALF = 64
CONV_HALO = 16
ROUTE_LANES = 128
VMEM_LIMIT = 56 * 1024 * 1024

F32 = jnp.float32
BF16 = jnp.bfloat16


def _params(sem, vmem=VMEM_LIMIT):
    return pltpu.CompilerParams(dimension_semantics=sem, vmem_limit_bytes=vmem)


def _seq_of_row(row, lay):
    batch, seq, _, dec_seq = lay
    n_prompt = batch * seq
    return jnp.where(row < n_prompt, row // seq, batch + (row - n_prompt) // dec_seq)


def _ada_body(c_ref, w_ref, b_ref, o_ref):
    c = c_ref[...]
    sc = c * jax.nn.sigmoid(c)
    o_ref[0] = jnp.dot(sc, w_ref[0], preferred_element_type=F32) + b_ref[0]


def _ada_mod(c_pad, ada_w, ada_b):
    ns8 = c_pad.shape[0]
    tn = 1024
    nj = 6 * D_MODEL // tn
    return pl.pallas_call(
        _ada_body,
        grid=(DEPTH, nj),
        in_specs=[
            pl.BlockSpec((ns8, D_MODEL), lambda l, j: (0, 0)),
            pl.BlockSpec((1, D_MODEL, tn), lambda l, j: (l, 0, j)),
            pl.BlockSpec((1, 1, tn), lambda l, j: (l, 0, j)),
        ],
        out_specs=pl.BlockSpec((1, ns8, tn), lambda l, j: (l, 0, j)),
        out_shape=jax.ShapeDtypeStruct((DEPTH, ns8, 6 * D_MODEL), F32),
        compiler_params=_params(("arbitrary", "arbitrary")),
    )(c_pad, ada_w, ada_b.reshape(DEPTH, 1, 6 * D_MODEL))


def _proj_in_body(x_ref, shift_ref, scale_ref, g_ref, wqkv_ref, wga_ref, wgb_ref,
                  qkv_ref, u_ref, h_scr):
    j = pl.program_id(1)

    @pl.when(j == 0)
    def _():
        x = x_ref[...]
        ms = jnp.mean(x * x, axis=-1, keepdims=True)
        y = x * lax.rsqrt(ms + NORM_EPS) * g_ref[...]
        h_scr[...] = (y * (1.0 + scale_ref[0]) + shift_ref[0]).astype(BF16)

    @pl.when(j < 3)
    def _():
        r = jnp.dot(h_scr[...], wqkv_ref[...], preferred_element_type=F32)
        r = r * jnp.where(j == 0, HEAD_DIM ** -0.5, 1.0).astype(F32)
        for hh in range(N_HEADS):
            qkv_ref[hh] = r[:, hh * HEAD_DIM:(hh + 1) * HEAD_DIM]

    @pl.when(j >= 3)
    def _():
        h = h_scr[...]
        a = jnp.dot(h, wga_ref[...], preferred_element_type=F32)
        b = jnp.dot(h, wgb_ref[...], preferred_element_type=F32)
        u_ref[...] = a * jax.nn.sigmoid(b)


def _proj_in(x, modr, norm_g, w_in_bf, lay, tm):
    t = x.shape[0]
    tn = ATTN_WIDTH
    tg = CONV_CH // 2
    ng = CONV_CH // tg
    nq = 3 * ATTN_WIDTH // tg

    def seq(i):
        return _seq_of_row(i * tm, lay)

    def gcol(j):
        return jnp.clip(j - 3, 0, ng - 1)

    return pl.pallas_call(
        _proj_in_body,
        grid=(t // tm, 3 + ng),
        in_specs=[
            pl.BlockSpec((tm, D_MODEL), lambda i, j: (i, 0)),
            pl.BlockSpec((1, 1, D_MODEL), lambda i, j: (seq(i) * 6 + 0, 0, 0)),
            pl.BlockSpec((1, 1, D_MODEL), lambda i, j: (seq(i) * 6 + 1, 0, 0)),
            pl.BlockSpec((1, D_MODEL), lambda i, j: (0, 0)),
            pl.BlockSpec((D_MODEL, tn), lambda i, j: (0, jnp.minimum(j, 2))),
            pl.BlockSpec((D_MODEL, tg), lambda i, j: (0, nq + gcol(j))),
            pl.BlockSpec((D_MODEL, tg), lambda i, j: (0, nq + ng + gcol(j))),
        ],
        out_specs=[
            pl.BlockSpec((N_HEADS, tm, HEAD_DIM), lambda i, j: (jnp.minimum(j, 2), i, 0)),
            pl.BlockSpec((tm, tg), lambda i, j: (i, gcol(j))),
        ],
        out_shape=[
            jax.ShapeDtypeStruct((3 * N_HEADS, t, HEAD_DIM), F32),
            jax.ShapeDtypeStruct((t, CONV_CH), F32),
        ],
        scratch_shapes=[pltpu.VMEM((tm, D_MODEL), BF16)],
        compiler_params=_params(("arbitrary", "arbitrary")),
    )(x, modr, modr, norm_g.reshape(1, D_MODEL), w_in_bf, w_in_bf, w_in_bf)


def _alibi_slopes():
    return jnp.exp2(-8.0 * jnp.arange(1, N_HEADS + 1, dtype=F32) / N_HEADS)


def _attn_bias():
    i = jnp.arange(ATT_TQ)[:, None]
    j = jnp.arange(ATT_TK)[None, :]
    out = []
    for _, d in DILATED_CONFIGS:
        for v in range(3):
            rel = jnp.abs(j - ATT_HALF * v - i)
            dist = (d * rel).astype(F32)
            b = -_alibi_slopes()[:, None, None] * dist[None]
            out.append(jnp.where((rel <= ATT_HALF)[None], b, MASK_VALUE))
    return jnp.stack(out, axis=1)


def _attn_body(q_ref, k_ref, v_ref, bias_ref, *rest, seq_len):
    o_ref, m_scr, l_scr = rest[-3:]
    nc = len(DILATED_CONFIGS)
    for c, (_, d) in enumerate(DILATED_CONFIGS):
        sub_len = seq_len // d
        nt = sub_len // ATT_TQ

        def tile(n, carry, c=c, d=d, sub_len=sub_len, nt=nt):
            r = lax.div(n, nt)
            l0 = lax.rem(n, nt) * ATT_TQ
            k0 = jnp.clip(l0 - ATT_HALF, 0, sub_len - ATT_TK)
            var = lax.div(l0 - k0, ATT_HALF)
            qrows = pl.ds(r + l0 * d, ATT_TQ, stride=d)
            krows = pl.ds(r + k0 * d, ATT_TK, stride=d)
            q = q_ref[0, qrows, :].astype(BF16)
            k = k_ref[0, krows, :].astype(BF16)
            v = v_ref[0, krows, :].astype(BF16)
            s = lax.dot_general(q, k, (((1,), (1,)), ((), ())), preferred_element_type=F32)
            s = s + bias_ref[0, 3 * c + var]
            m = jnp.max(s, axis=-1, keepdims=True)
            p = jnp.exp(s - m)
            l = jnp.sum(p, axis=-1, keepdims=True)
            o = jnp.dot(p.astype(BF16), v, preferred_element_type=F32)
            if c > 0:
                m_old = m_scr[qrows, :][:, :1]
                l_old = l_scr[qrows, :][:, :1]
                m_new = jnp.maximum(m_old, m)
                a_old = jnp.exp(m_old - m_new)
                a_cur = jnp.exp(m - m_new)
                o = a_old * o_ref[qrows, :] + a_cur * o
                l = a_old * l_old + a_cur * l
                m = m_new
            if c < nc - 1:
                o_ref[qrows, :] = o
                m_scr[qrows, :] = jnp.broadcast_to(m, (ATT_TQ, LANES))
                l_scr[qrows, :] = jnp.broadcast_to(l, (ATT_TQ, LANES))
            else:
                o_ref[qrows, :] = o / l
            return carry

        lax.fori_loop(0, d * nt, tile, 0)


def _attention(qkv, bias, lay, prev=None, *, group):
    batch, seq, dec_batch, dec_seq = lay
    t = qkv.shape[1]
    if group == 0:
        nb, s_len, blk0 = batch, seq, 0
    else:
        assert (batch * seq) % dec_seq == 0
        nb, s_len, blk0 = dec_batch, dec_seq, batch * seq // dec_seq
    assert s_len % (ATT_TK * DILATED_CONFIGS[-1][1]) == 0

    def hm(part):
        return lambda b, h: (part * N_HEADS + h, blk0 + b, 0)

    in_specs = [
        pl.BlockSpec((1, s_len, HEAD_DIM), hm(0)),
        pl.BlockSpec((1, s_len, HEAD_DIM), hm(1)),
        pl.BlockSpec((1, s_len, HEAD_DIM), hm(2)),
        pl.BlockSpec((1, bias.shape[1], ATT_TQ, ATT_TK), lambda b, h: (h, 0, 0, 0)),
    ]
    args = [qkv, qkv, qkv, bias]
    aliases = {}
    if prev is not None:
        in_specs.append(pl.BlockSpec(memory_space=pl.ANY))
        args.append(prev)
        aliases = {4: 0}
    return pl.pallas_call(
        functools.partial(_attn_body, seq_len=s_len),
        grid=(nb, N_HEADS),
        in_specs=in_specs,
        out_specs=pl.BlockSpec((s_len, HEAD_DIM), lambda b, h: (blk0 + b, h)),
        out_shape=jax.ShapeDtypeStruct((t, ATTN_WIDTH), F32),
        scratch_shapes=[pltpu.VMEM((s_len, LANES), F32), pltpu.VMEM((s_len, LANES), F32)],
        input_output_aliases=aliases,
        compiler_params=_params(("arbitrary", "arbitrary")),
    )(*args)


def _conv_body(prev_ref, cur_ref, next_ref, w_ref, cb_ref, lg_ref, lb_ref, og_ref, o_ref,
               pad_scr, cv_scr, *, lay, ts):
    batch, seq, dec_batch, dec_seq = lay
    n_prompt = batch * seq
    row0 = pl.program_id(0) * ts
    row1 = row0 + ts
    pos0 = jnp.where(row0 < n_prompt, row0 % seq, (row0 - n_prompt) % dec_seq)
    pos1 = jnp.where(row1 <= n_prompt, row1 % seq, (row1 - n_prompt) % dec_seq)
    pad_scr[0:CONV_HALO, :] = jnp.where(pos0 != 0, prev_ref[...], 0.0)
    pad_scr[CONV_HALO:CONV_HALO + ts, :] = cur_ref[...]
    pad_scr[CONV_HALO + ts:, :] = jnp.where(pos1 != 0, next_ref[...], 0.0)

    rc = 64
    base = CONV_HALO - CONV_WIDTH // 2

    def lane_chunk(c, carry):
        c0 = pl.multiple_of(c * LANES, LANES)
        for r in range(ts // rc):
            acc = jnp.zeros((rc, LANES), F32)
            for j in range(CONV_WIDTH):
                acc = acc + (pad_scr[pl.ds(base + r * rc + j, rc), pl.ds(c0, LANES)]
                             * w_ref[pl.ds(j, 1), pl.ds(c0, LANES)])
            cv_scr[pl.ds(r * rc, rc), pl.ds(c0, LANES)] = acc
        return carry

    lax.fori_loop(0, CONV_CH // LANES, lane_chunk, 0)

    rn = 32

    def row_chunk(r, carry):
        r0 = pl.multiple_of(r * rn, rn)
        u = cv_scr[pl.ds(r0, rn), :] + cb_ref[...]
        mu = jnp.mean(u, axis=-1, keepdims=True)
        var = jnp.mean(jnp.square(u - mu), axis=-1, keepdims=True)
        y = (u - mu) * lax.rsqrt(var + NORM_EPS) * lg_ref[...] + lb_ref[...]
        y = y * jax.nn.sigmoid(y)
        ms = jnp.mean(y * y, axis=-1, keepdims=True)
        o_ref[pl.ds(r0, rn), :] = (y * lax.rsqrt(ms + NORM_EPS) * og_ref[...]).astype(o_ref.dtype)
        return carry

    lax.fori_loop(0, ts // rn, row_chunk, 0)


def _conv(u, conv_w, conv_b, ln_g, ln_b, out_g, lay, ts):
    t = u.shape[0]
    hb = ts // CONV_HALO
    nhb = t // CONV_HALO
    row = lambda a: a.reshape(1, CONV_CH)
    return pl.pallas_call(
        functools.partial(_conv_body, lay=lay, ts=ts),
        grid=(t // ts,),
        in_specs=[
            pl.BlockSpec((CONV_HALO, CONV_CH), lambda i: (jnp.maximum(i * hb - 1, 0), 0)),
            pl.BlockSpec((ts, CONV_CH), lambda i: (i, 0)),
            pl.BlockSpec((CONV_HALO, CONV_CH), lambda i: (jnp.minimum((i + 1) * hb, nhb - 1), 0)),
            pl.BlockSpec((CONV_WIDTH, CONV_CH), lambda i: (0, 0)),
            pl.BlockSpec((1, CONV_CH), lambda i: (0, 0)),
            pl.BlockSpec((1, CONV_CH), lambda i: (0, 0)),
            pl.BlockSpec((1, CONV_CH), lambda i: (0, 0)),
            pl.BlockSpec((1, CONV_CH), lambda i: (0, 0)),
        ],
        out_specs=pl.BlockSpec((ts, CONV_CH), lambda i: (i, 0)),
        out_shape=jax.ShapeDtypeStruct((t, CONV_CH), BF16),
        scratch_shapes=[pltpu.VMEM((ts + 2 * CONV_HALO, CONV_CH), F32),
                        pltpu.VMEM((ts, CONV_CH), F32)],
        compiler_params=_params(("arbitrary",)),
    )(u, u, u, conv_w, row(conv_b), row(ln_g), row(ln_b), row(out_g))


def _proj_out_body(attn_ref, conv_ref, x_ref, ag_ref, wout_ref, gate_ref, g2_ref, shift_ref,
                   scale_ref, wr_ref, br_ref, xmid_ref, h_ref, ri_ref, rf_ref):
    a = attn_ref[...]
    ms = jnp.mean(a * a, axis=-1, keepdims=True)
    an = (a * lax.rsqrt(ms + NORM_EPS) * ag_ref[...]).astype(BF16)
    o = jnp.dot(an, wout_ref[0:ATTN_WIDTH, :], preferred_element_type=F32)
    o = o + jnp.dot(conv_ref[...], wout_ref[ATTN_WIDTH:, :], preferred_element_type=F32)
    x = x_ref[...] + gate_ref[0] * o
    xmid_ref[...] = x
    ms2 = jnp.mean(x * x, axis=-1, keepdims=True)
    y = x * lax.rsqrt(ms2 + NORM_EPS) * g2_ref[...]
    hb = (y * (1.0 + scale_ref[0]) + shift_ref[0]).astype(BF16)
    h_ref[...] = hb

    lg = jnp.dot(hb, wr_ref[...], preferred_element_type=F32) + br_ref[...]
    lane = lax.broadcasted_iota(jnp.int32, lg.shape, 1)
    lane_f = lane.astype(F32)
    neg = jnp.float32(-jnp.inf)
    big = jnp.float32(ROUTE_LANES)
    gmask = lane < N_GROUPS
    lgm = jnp.where(gmask, lg, neg)
    gmax = jnp.max(lgm, axis=-1, keepdims=True)
    gidx = jnp.min(jnp.where(lgm == gmax, lane_f, big), axis=-1, keepdims=True).astype(jnp.int32)
    gsum = jnp.sum(jnp.where(gmask, jnp.exp(lg - gmax), 0.0), axis=-1, keepdims=True)
    pg = 1.0 / gsum
    lo = N_GROUPS + gidx * EXPERTS_PER_GROUP
    emask = (lane >= lo) & (lane < lo + EXPERTS_PER_GROUP)
    le1 = jnp.where(emask, lg, neg)
    e1 = jnp.max(le1, axis=-1, keepdims=True)
    i1 = jnp.min(jnp.where(le1 == e1, lane_f, big), axis=-1, keepdims=True)
    le2 = jnp.where(lane_f == i1, neg, le1)
    e2 = jnp.max(le2, axis=-1, keepdims=True)
    i2 = jnp.min(jnp.where(le2 == e2, lane_f, big), axis=-1, keepdims=True)
    r = jnp.exp(e2 - e1)
    w1 = pg / (1.0 + r)
    w2 = pg * r / (1.0 + r)
    id1 = (i1 - N_GROUPS).astype(jnp.int32)
    id2 = (i2 - N_GROUPS).astype(jnp.int32)
    ri_ref[...] = jnp.where(lane == 0, id1, jnp.where(lane == 1, id2, 0))
    rf_ref[...] = jnp.where(lane == 0, w1, jnp.where(lane == 1, w2, 0.0))


def _proj_out(attn, convn, x, modr, attn_g, w_out_bf, norm2_g, wr, br, lay, tm):
    t = x.shape[0]

    def seq(i):
        return _seq_of_row(i * tm, lay)

    row = lambda a, n: a.reshape(1, n)
    return pl.pallas_call(
        _proj_out_body,
        grid=(t // tm,),
        in_specs=[
            pl.BlockSpec((tm, ATTN_WIDTH), lambda i: (i, 0)),
            pl.BlockSpec((tm, CONV_CH), lambda i: (i, 0)),
            pl.BlockSpec((tm, D_MODEL), lambda i: (i, 0)),
            pl.BlockSpec((1, ATTN_WIDTH), lambda i: (0, 0)),
            pl.BlockSpec((D_MODEL, D_MODEL), lambda i: (0, 0)),
            pl.BlockSpec((1, 1, D_MODEL), lambda i: (seq(i) * 6 + 2, 0, 0)),
            pl.BlockSpec((1, D_MODEL), lambda i: (0, 0)),
            pl.BlockSpec((1, 1, D_MODEL), lambda i: (seq(i) * 6 + 3, 0, 0)),
            pl.BlockSpec((1, 1, D_MODEL), lambda i: (seq(i) * 6 + 4, 0, 0)),
            pl.BlockSpec((D_MODEL, ROUTE_LANES), lambda i: (0, 0)),
            pl.BlockSpec((1, ROUTE_LANES), lambda i: (0, 0)),
        ],
        out_specs=[
            pl.BlockSpec((tm, D_MODEL), lambda i: (i, 0)),
            pl.BlockSpec((tm, D_MODEL), lambda i: (i, 0)),
            pl.BlockSpec((tm, ROUTE_LANES), lambda i: (i, 0)),
            pl.BlockSpec((tm, ROUTE_LANES), lambda i: (i, 0)),
        ],
        out_shape=[
            jax.ShapeDtypeStruct((t, D_MODEL), F32),
            jax.ShapeDtypeStruct((t, D_MODEL), BF16),
            jax.ShapeDtypeStruct((t, ROUTE_LANES), jnp.int32),
            jax.ShapeDtypeStruct((t, ROUTE_LANES), F32),
        ],
        compiler_params=_params(("arbitrary",)),
    )(attn, convn, x, row(attn_g, ATTN_WIDTH), w_out_bf, modr, row(norm2_g, D_MODEL), modr, modr,
      wr, br)


def _dispatch(experts, gates, tm):
    t = experts.shape[0]
    a = t * TOP_K
    nb = -(-(a + N_EXPERTS * (tm - 1)) // tm)
    flat_e = experts.reshape(-1)
    flat_tok = jnp.repeat(jnp.arange(t, dtype=jnp.int32), TOP_K)
    flat_w = gates.reshape(-1)
    order = jnp.argsort(flat_e)
    sorted_e = flat_e[order]
    counts = jnp.bincount(flat_e, length=N_EXPERTS)
    starts = jnp.cumsum(counts) - counts
    padded = ((counts + tm - 1) // tm) * tm
    padded_ends = jnp.cumsum(padded)
    padded_starts = padded_ends - padded
    dest = (padded_starts[sorted_e] + (jnp.arange(a) - starts[sorted_e])).astype(jnp.int32)
    row_tok = jnp.zeros((nb * tm,), jnp.int32).at[dest].set(flat_tok[order])
    row_w = jnp.zeros((nb * tm,), F32).at[dest].set(flat_w[order])
    pos = jnp.zeros((a,), jnp.int32).at[order].set(dest).reshape(t, TOP_K)
    block_expert = jnp.minimum(
        jnp.searchsorted(padded_ends, jnp.arange(nb) * tm, side='right'), N_EXPERTS - 1).astype(jnp.int32)
    n_valid = (padded_ends[-1] // tm).astype(jnp.int32).reshape(1)
    return row_tok, row_w, pos, block_expert, n_valid


def _gate_up_body(be_ref, nv_ref, xs_ref, wg_ref, wu_ref, a_ref):
    i = pl.program_id(1)

    @pl.when(i < nv_ref[0])
    def _():
        x = xs_ref[...]
        g = jnp.dot(x, wg_ref[0, 0].astype(BF16), preferred_element_type=F32)
        u = jnp.dot(x, wu_ref[0, 0].astype(BF16), preferred_element_type=F32)
        a_ref[...] = (g * jax.nn.sigmoid(g) * u).astype(a_ref.dtype)

    @pl.when(i >= nv_ref[0])
    def _():
        a_ref[...] = jnp.zeros_like(a_ref)


def _gate_up(xs, e_gate, e_up, block_expert, n_valid, layer, tm):
    p = xs.shape[0]
    tn = D_EXPERT // 2
    w_spec = pl.BlockSpec((1, 1, D_MODEL, tn), lambda j, i, be, nv: (layer, be[i], 0, j))
    return pl.pallas_call(
        _gate_up_body,
        grid_spec=pltpu.PrefetchScalarGridSpec(
            num_scalar_prefetch=2,
            grid=(D_EXPERT // tn, p // tm),
            in_specs=[
                pl.BlockSpec((tm, D_MODEL), lambda j, i, be, nv: (jnp.minimum(i, nv[0] - 1), 0)),
                w_spec, w_spec,
            ],
            out_specs=pl.BlockSpec((tm, tn), lambda j, i, be, nv: (i, j)),
        ),
        out_shape=jax.ShapeDtypeStruct((p, D_EXPERT), BF16),
        compiler_params=_params(("arbitrary", "arbitrary")),
    )(block_expert, n_valid, xs, e_gate, e_up)


def _down_body(be_ref, nv_ref, a_ref, wd_ref, rw_ref, y_ref):
    i = pl.program_id(1)

    @pl.when(i < nv_ref[0])
    def _():
        y = jnp.dot(a_ref[...], wd_ref[0, 0].astype(BF16), preferred_element_type=F32)
        rw = rw_ref[...]
        for c in range(y.shape[1] // LANES):
            y_ref[:, c * LANES:(c + 1) * LANES] = y[:, c * LANES:(c + 1) * LANES] * rw

    @pl.when(i >= nv_ref[0])
    def _():
        y_ref[...] = jnp.zeros_like(y_ref)


def _down(a, e_down, row_w_lanes, block_expert, n_valid, layer, tm):
    p = a.shape[0]
    tn = D_MODEL // 2
    return pl.pallas_call(
        _down_body,
        grid_spec=pltpu.PrefetchScalarGridSpec(
            num_scalar_prefetch=2,
            grid=(D_MODEL // tn, p // tm),
            in_specs=[
                pl.BlockSpec((tm, D_EXPERT), lambda j, i, be, nv: (jnp.minimum(i, nv[0] - 1), 0)),
                pl.BlockSpec((1, 1, D_EXPERT, tn), lambda j, i, be, nv: (layer, be[i], 0, j)),
                pl.BlockSpec((tm, LANES), lambda j, i, be, nv: (i, 0)),
            ],
            out_specs=pl.BlockSpec((tm, tn), lambda j, i, be, nv: (i, j)),
        ),
        out_shape=jax.ShapeDtypeStruct((p, D_MODEL), F32),
        compiler_params=_params(("arbitrary", "arbitrary")),
    )(block_expert, n_valid, a, e_down, row_w_lanes)


def _combine_body(x_ref, y0_ref, y1_ref, gate_ref, o_ref):
    o_ref[...] = x_ref[...] + gate_ref[0] * (y0_ref[...] + y1_ref[...])


def _combine(x, y0, y1, modr, lay, tm):
    t = x.shape[0]
    blk = pl.BlockSpec((tm, D_MODEL), lambda i: (i, 0))
    return pl.pallas_call(
        _combine_body,
        grid=(t // tm,),
        in_specs=[blk, blk, blk,
                  pl.BlockSpec((1, 1, D_MODEL), lambda i: (_seq_of_row(i * tm, lay) * 6 + 5, 0, 0))],
        out_specs=blk,
        out_shape=jax.ShapeDtypeStruct((t, D_MODEL), F32),
        compiler_params=_params(("arbitrary",)),
    )(x, y0, y1, modr)


def _final_body(x_ref, y0_ref, y1_ref, gate_ref, g_ref, op_ref, os_ref, *, n_prompt_blocks):
    i = pl.program_id(0)
    x = x_ref[...] + gate_ref[0] * (y0_ref[...] + y1_ref[...])
    ms = jnp.mean(x * x, axis=-1, keepdims=True)
    y = x * lax.rsqrt(ms + NORM_EPS) * g_ref[...]

    @pl.when(i < n_prompt_blocks)
    def _():
        op_ref[...] = y

    @pl.when(i >= n_prompt_blocks)
    def _():
        os_ref[...] = y


def _final(x, y0, y1, modr, final_g, lay, tm):
    batch, seq, dec_batch, dec_seq = lay
    t = x.shape[0]
    n_prompt = batch * seq
    npb = n_prompt // tm
    blk = pl.BlockSpec((tm, D_MODEL), lambda i: (i, 0))
    return pl.pallas_call(
        functools.partial(_final_body, n_prompt_blocks=npb),
        grid=(t // tm,),
        in_specs=[blk, blk, blk,
                  pl.BlockSpec((1, 1, D_MODEL), lambda i: (_seq_of_row(i * tm, lay) * 6 + 5, 0, 0)),
                  pl.BlockSpec((1, D_MODEL), lambda i: (0, 0))],
        out_specs=[
            pl.BlockSpec((tm, D_MODEL), lambda i: (jnp.minimum(i, npb - 1), 0)),
            pl.BlockSpec((tm, D_MODEL), lambda i: (jnp.maximum(i - npb, 0), 0)),
        ],
        out_shape=[
            jax.ShapeDtypeStruct((n_prompt, D_MODEL), F32),
            jax.ShapeDtypeStruct((t - n_prompt, D_MODEL), F32),
        ],
        compiler_params=_params(("arbitrary",)),
    )(x, y0, y1, modr, final_g.reshape(1, D_MODEL))


def _trunk(x, c_all, lay, ada_w, ada_b, norm1_g, w_in, conv_w, conv_b, conv_ln_g, conv_ln_b,
           attn_out_g, conv_out_g, w_out, norm2_g, rg_w, rg_b, re_w, re_b, e_gate, e_up, e_down,
           final_g, *, tm_in=512, tm_out=256, ts_conv=256, tm_moe=512, tm_res=512):
    batch, seq, dec_batch, dec_seq = lay
    t = x.shape[0]
    ns = c_all.shape[0]
    ns8 = -(-ns // 8) * 8
    c_pad = jnp.zeros((ns8, D_MODEL), F32).at[:ns].set(c_all)
    mod = _ada_mod(c_pad, ada_w, ada_b)
    bias = _attn_bias()
    n_route = N_GROUPS + N_EXPERTS
    y_prompt = y_sample = None
    for l in range(DEPTH):
        modr = mod[l, :ns].reshape(ns * 6, 1, D_MODEL)
        qkv, u = _proj_in(x, modr, norm1_g[l], w_in[l].astype(BF16), lay, tm_in)
        attn = _attention(qkv, bias, lay, group=0)
        attn = _attention(qkv, bias, lay, attn, group=1)
        convn = _conv(u, conv_w[l], conv_b[l], conv_ln_g[l], conv_ln_b[l], conv_out_g[l], lay, ts_conv)
        wr = jnp.zeros((D_MODEL, ROUTE_LANES), F32).at[:, :n_route].set(
            jnp.concatenate([rg_w[l], re_w[l]], axis=1)).astype(BF16)
        br = jnp.zeros((1, ROUTE_LANES), F32).at[0, :n_route].set(jnp.concatenate([rg_b[l], re_b[l]]))
        x_mid, h2, ri, rf = _proj_out(attn, convn, x, modr, attn_out_g[l], w_out[l].astype(BF16),
                                      norm2_g[l], wr, br, lay, tm_out)
        row_tok, row_w, pos, block_expert, n_valid = _dispatch(ri[:, :TOP_K], rf[:, :TOP_K], tm_moe)
        xs = h2.at[row_tok].get(mode='promise_in_bounds')
        act = _gate_up(xs, e_gate, e_up, block_expert, n_valid, l, tm_moe)
        row_w_lanes = jnp.broadcast_to(row_w[:, None], (row_w.shape[0], LANES))
        ys = _down(act, e_down, row_w_lanes, block_expert, n_valid, l, tm_moe)
        y0 = ys.at[pos[:, 0]].get(mode='promise_in_bounds')
        y1 = ys.at[pos[:, 1]].get(mode='promise_in_bounds')
        if l < DEPTH - 1:
            x = _combine(x_mid, y0, y1, modr, lay, tm_res)
        else:
            y_prompt, y_sample = _final(x_mid, y0, y1, modr, final_g, lay, tm_res)
    return (y_prompt.reshape(batch, seq, D_MODEL), y_sample.reshape(dec_batch, dec_seq, D_MODEL))


def kernel(x_prompt, x_sample, c_prompt, c_sample, ada_w, ada_b, norm1_g, w_in, conv_w, conv_b, conv_ln_g, conv_ln_b, attn_out_g, conv_out_g, w_out, norm2_g, rg_w, rg_b, re_w, re_b, e_gate, e_up, e_down, final_g):
    batch, seq, _ = x_prompt.shape
    dec_batch, dec_seq, _ = x_sample.shape
    lay = (batch, seq, dec_batch, dec_seq)
    x = jnp.concatenate([x_prompt.reshape(batch * seq, D_MODEL),
                         x_sample.reshape(dec_batch * dec_seq, D_MODEL)], axis=0)
    c_all = jnp.concatenate([c_prompt, c_sample], axis=0)
    return _trunk(x, c_all, lay, ada_w, ada_b, norm1_g, w_in, conv_w, conv_b, conv_ln_g, conv_ln_b,
                  attn_out_g, conv_out_g, w_out, norm2_g, rg_w, rg_b, re_w, re_b,
                  e_gate, e_up, e_down, final_g)
```

```python
import functools
import math

import jax
import jax.numpy as jnp
from jax import lax
from jax.experimental import pallas as pl
from jax.experimental.pallas import tpu as pltpu

D_MODEL = 2048
DEPTH = 4
N_HEADS = 8
HEAD_DIM = 128
ATTN_WIDTH = N_HEADS * HEAD_DIM
CONV_CH = D_MODEL - ATTN_WIDTH
CONV_WIDTH = 31
DILATED_CONFIGS = ((128, 1), (512, 4), (2048, 16))
N_GROUPS = 4
EXPERTS_PER_GROUP = 8
N_EXPERTS = N_GROUPS * EXPERTS_PER_GROUP
TOP_K = 2
D_EXPERT = 1024
NORM_EPS = 1e-6
MASK_VALUE = -1e30

LANES = 128
ATT_TQ = 128
ATT_TK = 256
ATT_HALF = 64
ATT_STAGE = 256
ATT_UNROLL = 8
CONV_HALO = 16
ROUTE_LANES = 128
VMEM_LIMIT = 56 * 1024 * 1024

F32 = jnp.float32
BF16 = jnp.bfloat16


def _params(sem, vmem=VMEM_LIMIT):
    return pltpu.CompilerParams(dimension_semantics=sem, vmem_limit_bytes=vmem)


def _seq_of_row(row, lay):
    batch, seq, _, dec_seq = lay
    n_prompt = batch * seq
    return jnp.where(row < n_prompt, row // seq, batch + (row - n_prompt) // dec_seq)


def _ada_body(c_ref, w_ref, b_ref, o_ref):
    c = c_ref[...]
    sc = c * jax.nn.sigmoid(c)
    o_ref[0] = jnp.dot(sc, w_ref[0], preferred_element_type=F32) + b_ref[0]


def _ada_mod(c_pad, ada_w, ada_b):
    ns8 = c_pad.shape[0]
    tn = 1024
    nj = 6 * D_MODEL // tn
    return pl.pallas_call(
        _ada_body,
        grid=(DEPTH, nj),
        in_specs=[
            pl.BlockSpec((ns8, D_MODEL), lambda l, j: (0, 0)),
            pl.BlockSpec((1, D_MODEL, tn), lambda l, j: (l, 0, j)),
            pl.BlockSpec((1, 1, tn), lambda l, j: (l, 0, j)),
        ],
        out_specs=pl.BlockSpec((1, ns8, tn), lambda l, j: (l, 0, j)),
        out_shape=jax.ShapeDtypeStruct((DEPTH, ns8, 6 * D_MODEL), F32),
        compiler_params=_params(("arbitrary", "arbitrary")),
    )(c_pad, ada_w, ada_b.reshape(DEPTH, 1, 6 * D_MODEL))


def _proj_in_body(x_ref, shift_ref, scale_ref, g_ref, wqkv_ref, wga_ref, wgb_ref,
                  qkv_ref, u_ref, h_scr):
    j = pl.program_id(1)

    @pl.when(j == 0)
    def _():
        x = x_ref[...]
        ms = jnp.mean(x * x, axis=-1, keepdims=True)
        y = x * lax.rsqrt(ms + NORM_EPS) * g_ref[...]
        h_scr[...] = (y * (1.0 + scale_ref[0]) + shift_ref[0]).astype(BF16)

    @pl.when(j < 3)
    def _():
        r = jnp.dot(h_scr[...], wqkv_ref[...], preferred_element_type=F32)
        r = r * jnp.where(j == 0, HEAD_DIM ** -0.5, 1.0).astype(F32)
        for hh in range(N_HEADS):
            qkv_ref[hh] = r[:, hh * HEAD_DIM:(hh + 1) * HEAD_DIM]

    @pl.when(j >= 3)
    def _():
        h = h_scr[...]
        a = jnp.dot(h, wga_ref[...], preferred_element_type=F32)
        b = jnp.dot(h, wgb_ref[...], preferred_element_type=F32)
        u_ref[...] = a * jax.nn.sigmoid(b)


def _proj_in(x, modr, norm_g, w_in_bf, lay, tm):
    t = x.shape[0]
    tn = ATTN_WIDTH
    tg = CONV_CH // 2
    ng = CONV_CH // tg
    nq = 3 * ATTN_WIDTH // tg

    def seq(i):
        return _seq_of_row(i * tm, lay)

    def gcol(j):
        return jnp.clip(j - 3, 0, ng - 1)

    return pl.pallas_call(
        _proj_in_body,
        grid=(t // tm, 3 + ng),
        in_specs=[
            pl.BlockSpec((tm, D_MODEL), lambda i, j: (i, 0)),
            pl.BlockSpec((1, 1, D_MODEL), lambda i, j: (seq(i) * 6 + 0, 0, 0)),
            pl.BlockSpec((1, 1, D_MODEL), lambda i, j: (seq(i) * 6 + 1, 0, 0)),
            pl.BlockSpec((1, D_MODEL), lambda i, j: (0, 0)),
            pl.BlockSpec((D_MODEL, tn), lambda i, j: (0, jnp.minimum(j, 2))),
            pl.BlockSpec((D_MODEL, tg), lambda i, j: (0, nq + gcol(j))),
            pl.BlockSpec((D_MODEL, tg), lambda i, j: (0, nq + ng + gcol(j))),
        ],
        out_specs=[
            pl.BlockSpec((N_HEADS, tm, HEAD_DIM), lambda i, j: (jnp.minimum(j, 2), i, 0)),
            pl.BlockSpec((tm, tg), lambda i, j: (i, gcol(j))),
        ],
        out_shape=[
            jax.ShapeDtypeStruct((3 * N_HEADS, t, HEAD_DIM), F32),
            jax.ShapeDtypeStruct((t, CONV_CH), F32),
        ],
        scratch_shapes=[pltpu.VMEM((tm, D_MODEL), BF16)],
        compiler_params=_params(("arbitrary", "arbitrary")),
    )(x, modr, modr, norm_g.reshape(1, D_MODEL), w_in_bf, w_in_bf, w_in_bf)


def _alibi_slopes():
    return jnp.exp2(-8.0 * jnp.arange(1, N_HEADS + 1, dtype=F32) / N_HEADS)


def _attn_bias():
    i = jnp.arange(ATT_TQ)[:, None]
    j = jnp.arange(ATT_TK)[None, :]
    out = []
    for _, d in DILATED_CONFIGS:
        for v in range(3):
            rel = jnp.abs(j - ATT_HALF * v - i)
            dist = (d * rel).astype(F32)
            b = -_alibi_slopes()[:, None, None] * dist[None]
            out.append(jnp.where((rel <= ATT_HALF)[None], b, MASK_VALUE))
    return jnp.stack(out, axis=1)


def _strided_rows(start, size, stride):
    return pl.ds(start, size, stride=stride) if stride > 1 else pl.ds(start, size)


def _attn_body(q_ref, k_ref, v_ref, bias_ref, *rest, seq_len):
    o_ref, lse_scr, qs, ks, vs = rest[-5:]
    order = sorted(range(len(DILATED_CONFIGS)), key=lambda c: -DILATED_CONFIGS[c][1])
    for step, c in enumerate(order):
        d = DILATED_CONFIGS[c][1]
        sub_len = seq_len // d
        first, last = step == 0, step == len(order) - 1

        nstage = sub_len // ATT_STAGE
        nt = sub_len // ATT_TQ

        def stage(n, cr, d=d, sub_len=sub_len, nstage=nstage):
            r = lax.div(n, nstage)
            ch = lax.rem(n, nstage)
            dst = pl.ds(pl.multiple_of(n * ATT_STAGE, ATT_STAGE), ATT_STAGE)
            src = _strided_rows(r + ch * (ATT_STAGE * d), ATT_STAGE, d)
            qs[dst, :] = q_ref[0, src, :].astype(BF16)
            ks[dst, :] = k_ref[0, src, :].astype(BF16)
            vs[dst, :] = v_ref[0, src, :].astype(BF16)
            return cr

        lax.fori_loop(0, d * nstage, stage, 0, unroll=2)

        def tile(n, ct, c=c, d=d, sub_len=sub_len, nt=nt, first=first, last=last):
            r = lax.div(n, nt)
            l0 = lax.rem(n, nt) * ATT_TQ
            k0 = jnp.clip(l0 - ATT_HALF, 0, sub_len - ATT_TK)
            var = lax.div(l0 - k0, ATT_HALF)
            base = r * sub_len
            q = qs[pl.ds(pl.multiple_of(base + l0, ATT_TQ), ATT_TQ), :]
            k = ks[pl.ds(pl.multiple_of(base + k0, ATT_HALF), ATT_TK), :]
            v = vs[pl.ds(pl.multiple_of(base + k0, ATT_HALF), ATT_TK), :]
            s = lax.dot_general(q, k, (((1,), (1,)), ((), ())), preferred_element_type=F32)
            s = s + bias_ref[0, 3 * c + var]
            m = jnp.max(s, axis=-1, keepdims=True)
            p = jnp.exp(s - m)
            l = jnp.sum(p, axis=-1, keepdims=True)
            o = jnp.dot(p.astype(BF16), v, preferred_element_type=F32)
            mb = jnp.broadcast_to(m, (ATT_TQ, LANES))
            lb = jnp.broadcast_to(l, (ATT_TQ, LANES))
            rows = _strided_rows(r + l0 * d, ATT_TQ, d)
            if first:
                o_ref[rows, :] = o / lb
                lse_scr[rows, :] = mb + jnp.log(lb)
            else:
                la = lse_scr[rows, :]
                mx = jnp.maximum(la, mb)
                wa = jnp.exp(la - mx)
                wb = jnp.exp(mb - mx)
                den = wa + wb * lb
                o_ref[rows, :] = (wa * o_ref[rows, :] + wb * o) / den
                if not last:
                    lse_scr[rows, :] = mx + jnp.log(den)
            return ct

        lax.fori_loop(0, d * nt, tile, 0, unroll=ATT_UNROLL)


def _attention(qkv, bias, lay, prev=None, *, group):
    batch, seq, dec_batch, dec_seq = lay
    t = qkv.shape[1]
    if group == 0:
        nb, s_len, blk0 = batch, seq, 0
    else:
        assert (batch * seq) % dec_seq == 0
        nb, s_len, blk0 = dec_batch, dec_seq, batch * seq // dec_seq
    assert s_len % (ATT_TK * DILATED_CONFIGS[-1][1]) == 0

    def hm(part):
        return lambda b, h: (part * N_HEADS + h, blk0 + b, 0)

    in_specs = [
        pl.BlockSpec((1, s_len, HEAD_DIM), hm(0)),
        pl.BlockSpec((1, s_len, HEAD_DIM), hm(1)),
        pl.BlockSpec((1, s_len, HEAD_DIM), hm(2)),
        pl.BlockSpec((1, bias.shape[1], ATT_TQ, ATT_TK), lambda b, h: (h, 0, 0, 0)),
    ]
    args = [qkv, qkv, qkv, bias]
    aliases = {}
    if prev is not None:
        in_specs.append(pl.BlockSpec(memory_space=pl.ANY))
        args.append(prev)
        aliases = {4: 0}
    return pl.pallas_call(
        functools.partial(_attn_body, seq_len=s_len),
        grid=(nb, N_HEADS),
        in_specs=in_specs,
        out_specs=pl.BlockSpec((s_len, HEAD_DIM), lambda b, h: (blk0 + b, h)),
        out_shape=jax.ShapeDtypeStruct((t, ATTN_WIDTH), F32),
        scratch_shapes=[pltpu.VMEM((s_len, LANES), F32)] + [pltpu.VMEM((s_len, HEAD_DIM), BF16)] * 3,
        input_output_aliases=aliases,
        compiler_params=_params(("arbitrary", "arbitrary")),
    )(*args)


def _conv_body(prev_ref, cur_ref, next_ref, w_ref, cb_ref, lg_ref, lb_ref, og_ref, o_ref,
               pad_scr, cv_scr, *, lay, ts):
    batch, seq, dec_batch, dec_seq = lay
    n_prompt = batch * seq
    row0 = pl.program_id(0) * ts
    row1 = row0 + ts
    pos0 = jnp.where(row0 < n_prompt, row0 % seq, (row0 - n_prompt) % dec_seq)
    pos1 = jnp.where(row1 <= n_prompt, row1 % seq, (row1 - n_prompt) % dec_seq)
    nch = CONV_CH // LANES
    lanes = [slice(cc * LANES, (cc + 1) * LANES) for cc in range(nch)]
    for cc in range(nch):
        pad_scr[cc, 0:CONV_HALO, :] = jnp.where(pos0 != 0, prev_ref[:, lanes[cc]], 0.0)
        pad_scr[cc, CONV_HALO:CONV_HALO + ts, :] = cur_ref[:, lanes[cc]]
        pad_scr[cc, CONV_HALO + ts:, :] = jnp.where(pos1 != 0, next_ref[:, lanes[cc]], 0.0)

    rc = 128
    base = CONV_HALO - CONV_WIDTH // 2

    def lane_chunk(cc, carry):
        for r in range(ts // rc):
            for par in range(2):
                acc = jnp.zeros((rc // 2, LANES), F32)
                for j in range(CONV_WIDTH):
                    acc = acc + (pad_scr[cc, pl.ds(base + r * rc + par + j, rc // 2, stride=2), :]
                                 * w_ref[cc, pl.ds(j, 1), :])
                cv_scr[cc, pl.ds(r * rc + par, rc // 2, stride=2), :] = acc
        return carry

    lax.fori_loop(0, nch, lane_chunk, 0)

    rn = 32
    inv_c = 1.0 / CONV_CH

    def row_chunk(r, carry):
        r0 = pl.multiple_of(r * rn, rn)
        u = [cv_scr[cc, pl.ds(r0, rn), :] + cb_ref[:, lanes[cc]] for cc in range(nch)]
        mu = jnp.sum(sum(u), axis=-1, keepdims=True) * inv_c
        u = [a - mu for a in u]
        var = jnp.sum(sum(a * a for a in u), axis=-1, keepdims=True) * inv_c
        rstd = lax.rsqrt(var + NORM_EPS)
        y = [a * rstd * lg_ref[:, lanes[cc]] + lb_ref[:, lanes[cc]] for cc, a in enumerate(u)]
        y = [a * jax.nn.sigmoid(a) for a in y]
        ms = jnp.sum(sum(a * a for a in y), axis=-1, keepdims=True) * inv_c
        rr = lax.rsqrt(ms + NORM_EPS)
        for cc, a in enumerate(y):
            o_ref[pl.ds(r0, rn), lanes[cc]] = (a * rr * og_ref[:, lanes[cc]]).astype(o_ref.dtype)
        return carry

    lax.fori_loop(0, ts // rn, row_chunk, 0, unroll=4)


def _conv(u, conv_w, conv_b, ln_g, ln_b, out_g, lay, ts):
    t = u.shape[0]
    hb = ts // CONV_HALO
    nhb = t // CONV_HALO
    nch = CONV_CH // LANES
    w_slabs = conv_w.reshape(CONV_WIDTH, nch, LANES).transpose(1, 0, 2)
    row = lambda a: a.reshape(1, CONV_CH)
    return pl.pallas_call(
        functools.partial(_conv_body, lay=lay, ts=ts),
        grid=(t // ts,),
        in_specs=[
            pl.BlockSpec((CONV_HALO, CONV_CH), lambda i: (jnp.maximum(i * hb - 1, 0), 0)),
            pl.BlockSpec((ts, CONV_CH), lambda i: (i, 0)),
            pl.BlockSpec((CONV_HALO, CONV_CH), lambda i: (jnp.minimum((i + 1) * hb, nhb - 1), 0)),
            pl.BlockSpec((nch, CONV_WIDTH, LANES), lambda i: (0, 0, 0)),
            pl.BlockSpec((1, CONV_CH), lambda i: (0, 0)),
            pl.BlockSpec((1, CONV_CH), lambda i: (0, 0)),
            pl.BlockSpec((1, CONV_CH), lambda i: (0, 0)),
            pl.BlockSpec((1, CONV_CH), lambda i: (0, 0)),
        ],
        out_specs=pl.BlockSpec((ts, CONV_CH), lambda i: (i, 0)),
        out_shape=jax.ShapeDtypeStruct((t, CONV_CH), BF16),
        scratch_shapes=[pltpu.VMEM((nch, ts + 2 * CONV_HALO, LANES), F32),
                        pltpu.VMEM((nch, ts, LANES), F32)],
        compiler_params=_params(("arbitrary",)),
    )(u, u, u, w_slabs, row(conv_b), row(ln_g), row(ln_b), row(out_g))


def _proj_out_body(attn_ref, conv_ref, x_ref, ag_ref, wout_ref, gate_ref, g2_ref, shift_ref,
                   scale_ref, wr_ref, br_ref, xmid_ref, h_ref, ri_ref, rf_ref):
    a = attn_ref[...]
    ms = jnp.mean(a * a, axis=-1, keepdims=True)
    an = (a * lax.rsqrt(ms + NORM_EPS) * ag_ref[...]).astype(BF16)
    o = jnp.dot(an, wout_ref[0:ATTN_WIDTH, :], preferred_element_type=F32)
    o = o + jnp.dot(conv_ref[...], wout_ref[ATTN_WIDTH:, :], preferred_element_type=F32)
    x = x_ref[...] + gate_ref[0] * o
    xmid_ref[...] = x
    ms2 = jnp.mean(x * x, axis=-1, keepdims=True)
    y = x * lax.rsqrt(ms2 + NORM_EPS) * g2_ref[...]
    hb = (y * (1.0 + scale_ref[0]) + shift_ref[0]).astype(BF16)
    bits = pltpu.bitcast(hb.astype(F32), jnp.uint32)
    half = D_MODEL // 2
    h_ref[...] = (bits[:, :half] >> 16) | (bits[:, half:] & jnp.uint32(0xFFFF0000))

    lg = jnp.dot(hb, wr_ref[...], preferred_element_type=F32) + br_ref[...]
    lane = lax.broadcasted_iota(jnp.int32, lg.shape, 1)
    lane_f = lane.astype(F32)
    neg = jnp.float32(-jnp.inf)
    big = jnp.float32(ROUTE_LANES)
    gmask = lane < N_GROUPS
    lgm = jnp.where(gmask, lg, neg)
    gmax = jnp.max(lgm, axis=-1, keepdims=True)
    gidx = jnp.min(jnp.where(lgm == gmax, lane_f, big), axis=-1, keepdims=True).astype(jnp.int32)
    gsum = jnp.sum(jnp.where(gmask, jnp.exp(lg - gmax), 0.0), axis=-1, keepdims=True)
    pg = 1.0 / gsum
    lo = N_GROUPS + gidx * EXPERTS_PER_GROUP
    emask = (lane >= lo) & (lane < lo + EXPERTS_PER_GROUP)
    le1 = jnp.where(emask, lg, neg)
    e1 = jnp.max(le1, axis=-1, keepdims=True)
    i1 = jnp.min(jnp.where(le1 == e1, lane_f, big), axis=-1, keepdims=True)
    le2 = jnp.where(lane_f == i1, neg, le1)
    e2 = jnp.max(le2, axis=-1, keepdims=True)
    i2 = jnp.min(jnp.where(le2 == e2, lane_f, big), axis=-1, keepdims=True)
    r = jnp.exp(e2 - e1)
    w1 = pg / (1.0 + r)
    w2 = pg * r / (1.0 + r)
    id1 = (i1 - N_GROUPS).astype(jnp.int32)
    id2 = (i2 - N_GROUPS).astype(jnp.int32)
    ri_ref[...] = jnp.where(lane == 0, id1, jnp.where(lane == 1, id2, 0))
    rf_ref[...] = jnp.where(lane == 0, w1, jnp.where(lane == 1, w2, 0.0))


def _proj_out(attn, convn, x, modr, attn_g, w_out_bf, norm2_g, wr, br, lay, tm):
    t = x.shape[0]

    def seq(i):
        return _seq_of_row(i * tm, lay)

    row = lambda a, n: a.reshape(1, n)
    return pl.pallas_call(
        _proj_out_body,
        grid=(t // tm,),
        in_specs=[
            pl.BlockSpec((tm, ATTN_WIDTH), lambda i: (i, 0)),
            pl.BlockSpec((tm, CONV_CH), lambda i: (i, 0)),
            pl.BlockSpec((tm, D_MODEL), lambda i: (i, 0)),
            pl.BlockSpec((1, ATTN_WIDTH), lambda i: (0, 0)),
            pl.BlockSpec((D_MODEL, D_MODEL), lambda i: (0, 0)),
            pl.BlockSpec((1, 1, D_MODEL), lambda i: (seq(i) * 6 + 2, 0, 0)),
            pl.BlockSpec((1, D_MODEL), lambda i: (0, 0)),
            pl.BlockSpec((1, 1, D_MODEL), lambda i: (seq(i) * 6 + 3, 0, 0)),
            pl.BlockSpec((1, 1, D_MODEL), lambda i: (seq(i) * 6 + 4, 0, 0)),
            pl.BlockSpec((D_MODEL, ROUTE_LANES), lambda i: (0, 0)),
            pl.BlockSpec((1, ROUTE_LANES), lambda i: (0, 0)),
        ],
        out_specs=[
            pl.BlockSpec((tm, D_MODEL), lambda i: (i, 0)),
            pl.BlockSpec((tm, D_MODEL // 2), lambda i: (i, 0)),
            pl.BlockSpec((tm, ROUTE_LANES), lambda i: (i, 0)),
            pl.BlockSpec((tm, ROUTE_LANES), lambda i: (i, 0)),
        ],
        out_shape=[
            jax.ShapeDtypeStruct((t, D_MODEL), F32),
            jax.ShapeDtypeStruct((t, D_MODEL // 2), jnp.uint32),
            jax.ShapeDtypeStruct((t, ROUTE_LANES), jnp.int32),
            jax.ShapeDtypeStruct((t, ROUTE_LANES), F32),
        ],
        compiler_params=_params(("arbitrary",)),
    )(attn, convn, x, row(attn_g, ATTN_WIDTH), w_out_bf, modr, row(norm2_g, D_MODEL), modr, modr,
      wr, br)


def _dispatch(experts, tm):
    t = experts.shape[0]
    a = t * TOP_K
    nb = -(-(a + N_EXPERTS * (tm - 1)) // tm)
    flat_e = experts.reshape(-1)
    order = jnp.argsort(flat_e).astype(jnp.int32)
    rank = jnp.argsort(order).astype(jnp.int32)
    counts = jnp.sum(flat_e[:, None] == jnp.arange(N_EXPERTS)[None, :], axis=0).astype(jnp.int32)
    starts = jnp.cumsum(counts) - counts
    padded = ((counts + tm - 1) // tm) * tm
    padded_ends = jnp.cumsum(padded)
    padded_starts = padded_ends - padded
    shift = (padded_starts - starts).astype(jnp.int32)
    pos = (rank + shift[flat_e]).reshape(t, TOP_K)
    block_expert = jnp.minimum(
        jnp.sum(padded_ends[None, :] <= (jnp.arange(nb) * tm)[:, None], axis=1), N_EXPERTS - 1).astype(jnp.int32)
    row = jnp.arange(nb * tm, dtype=jnp.int32)
    row_e = jnp.repeat(block_expert, tm)
    sorted_pos = row - shift[row_e]
    valid = sorted_pos < (starts + counts)[row_e]
    src = order[jnp.clip(sorted_pos, 0, a - 1)]
    row_tok = jnp.where(valid, src // TOP_K, 0).astype(jnp.int32)
    n_valid = (padded_ends[-1] // tm).astype(jnp.int32).reshape(1)
    return row_tok, pos, block_expert, n_valid


def _unpack_rows(words):
    lo = pltpu.bitcast(words << 16, F32).astype(BF16)
    hi = pltpu.bitcast(words & jnp.uint32(0xFFFF0000), F32).astype(BF16)
    return lo, hi


def _gate_up_body(be_ref, nv_ref, xs_ref, wg_ref, wu_ref, a_ref):
    i = pl.program_id(1)
    half = D_MODEL // 2

    @pl.when(i < nv_ref[0])
    def _():
        lo, hi = _unpack_rows(xs_ref[...])
        g = (jnp.dot(lo, wg_ref[0, 0, :half, :].astype(BF16), preferred_element_type=F32)
             + jnp.dot(hi, wg_ref[0, 0, half:, :].astype(BF16), preferred_element_type=F32))
        u = (jnp.dot(lo, wu_ref[0, 0, :half, :].astype(BF16), preferred_element_type=F32)
             + jnp.dot(hi, wu_ref[0, 0, half:, :].astype(BF16), preferred_element_type=F32))
        a_ref[...] = (g * jax.nn.sigmoid(g) * u).astype(a_ref.dtype)

    @pl.when(i >= nv_ref[0])
    def _():
        a_ref[...] = jnp.zeros_like(a_ref)


def _gate_up(xs, e_gate, e_up, block_expert, n_valid, layer, tm):
    p = xs.shape[0]
    tn = D_EXPERT // 2
    w_spec = pl.BlockSpec((1, 1, D_MODEL, tn), lambda j, i, be, nv: (layer, be[i], 0, j))
    return pl.pallas_call(
        _gate_up_body,
        grid_spec=pltpu.PrefetchScalarGridSpec(
            num_scalar_prefetch=2,
            grid=(D_EXPERT // tn, p // tm),
            in_specs=[
                pl.BlockSpec((tm, D_MODEL // 2), lambda j, i, be, nv: (jnp.minimum(i, nv[0] - 1), 0)),
                w_spec, w_spec,
            ],
            out_specs=pl.BlockSpec((tm, tn), lambda j, i, be, nv: (i, j)),
        ),
        out_shape=jax.ShapeDtypeStruct((p, D_EXPERT), BF16),
        compiler_params=_params(("arbitrary", "arbitrary")),
    )(block_expert, n_valid, xs, e_gate, e_up)


def _down_body(be_ref, nv_ref, a_ref, wd_ref, y_ref):
    i = pl.program_id(1)

    @pl.when(i < nv_ref[0])
    def _():
        y_ref[...] = jnp.dot(a_ref[...], wd_ref[0, 0].astype(BF16), preferred_element_type=F32)

    @pl.when(i >= nv_ref[0])
    def _():
        y_ref[...] = jnp.zeros_like(y_ref)


def _down(a, e_down, block_expert, n_valid, layer, tm):
    p = a.shape[0]
    tn = D_MODEL // 2
    return pl.pallas_call(
        _down_body,
        grid_spec=pltpu.PrefetchScalarGridSpec(
            num_scalar_prefetch=2,
            grid=(D_MODEL // tn, p // tm),
            in_specs=[
                pl.BlockSpec((tm, D_EXPERT), lambda j, i, be, nv: (jnp.minimum(i, nv[0] - 1), 0)),
                pl.BlockSpec((1, 1, D_EXPERT, tn), lambda j, i, be, nv: (layer, be[i], 0, j)),
            ],
            out_specs=pl.BlockSpec((tm, tn), lambda j, i, be, nv: (i, j)),
        ),
        out_shape=jax.ShapeDtypeStruct((p, D_MODEL), F32),
        compiler_params=_params(("arbitrary", "arbitrary")),
    )(block_expert, n_valid, a, e_down)


def _moe_residual(x_ref, y0_ref, y1_ref, rf_ref, gate_ref):
    rf = rf_ref[...]
    return x_ref[...] + gate_ref[0] * (rf[:, 0:1] * y0_ref[...] + rf[:, 1:2] * y1_ref[...])


def _combine_body(x_ref, y0_ref, y1_ref, rf_ref, gate_ref, o_ref):
    o_ref[...] = _moe_residual(x_ref, y0_ref, y1_ref, rf_ref, gate_ref)


def _combine(x, y0, y1, rf, modr, lay, tm):
    t = x.shape[0]
    blk = pl.BlockSpec((tm, D_MODEL), lambda i: (i, 0))
    return pl.pallas_call(
        _combine_body,
        grid=(t // tm,),
        in_specs=[blk, blk, blk, pl.BlockSpec((tm, ROUTE_LANES), lambda i: (i, 0)),
                  pl.BlockSpec((1, 1, D_MODEL), lambda i: (_seq_of_row(i * tm, lay) * 6 + 5, 0, 0))],
        out_specs=blk,
        out_shape=jax.ShapeDtypeStruct((t, D_MODEL), F32),
        compiler_params=_params(("arbitrary",)),
    )(x, y0, y1, rf, modr)


def _final_body(x_ref, y0_ref, y1_ref, rf_ref, gate_ref, g_ref, op_ref, os_ref, *, n_prompt_blocks):
    i = pl.program_id(0)
    x = _moe_residual(x_ref, y0_ref, y1_ref, rf_ref, gate_ref)
    ms = jnp.mean(x * x, axis=-1, keepdims=True)
    y = x * lax.rsqrt(ms + NORM_EPS) * g_ref[...]

    @pl.when(i < n_prompt_blocks)
    def _():
        op_ref[...] = y

    @pl.when(i >= n_prompt_blocks)
    def _():
        os_ref[...] = y


def _final(x, y0, y1, rf, modr, final_g, lay, tm):
    batch, seq, dec_batch, dec_seq = lay
    t = x.shape[0]
    n_prompt = batch * seq
    npb = n_prompt // tm
    blk = pl.BlockSpec((tm, D_MODEL), lambda i: (i, 0))
    return pl.pallas_call(
        functools.partial(_final_body, n_prompt_blocks=npb),
        grid=(t // tm,),
        in_specs=[blk, blk, blk, pl.BlockSpec((tm, ROUTE_LANES), lambda i: (i, 0)),
                  pl.BlockSpec((1, 1, D_MODEL), lambda i: (_seq_of_row(i * tm, lay) * 6 + 5, 0, 0)),
                  pl.BlockSpec((1, D_MODEL), lambda i: (0, 0))],
        out_specs=[
            pl.BlockSpec((tm, D_MODEL), lambda i: (jnp.minimum(i, npb - 1), 0)),
            pl.BlockSpec((tm, D_MODEL), lambda i: (jnp.maximum(i - npb, 0), 0)),
        ],
        out_shape=[
            jax.ShapeDtypeStruct((n_prompt, D_MODEL), F32),
            jax.ShapeDtypeStruct((t - n_prompt, D_MODEL), F32),
        ],
        compiler_params=_params(("arbitrary",)),
    )(x, y0, y1, rf, modr, final_g.reshape(1, D_MODEL))


def _trunk(x, c_all, lay, ada_w, ada_b, norm1_g, w_in, conv_w, conv_b, conv_ln_g, conv_ln_b,
           attn_out_g, conv_out_g, w_out, norm2_g, rg_w, rg_b, re_w, re_b, e_gate, e_up, e_down,
           final_g, *, tm_in=512, tm_out=256, ts_conv=256, tm_moe=512, tm_res=512):
    batch, seq, dec_batch, dec_seq = lay
    t = x.shape[0]
    ns = c_all.shape[0]
    ns8 = -(-ns // 8) * 8
    c_pad = jnp.zeros((ns8, D_MODEL), F32).at[:ns].set(c_all)
    mod = _ada_mod(c_pad, ada_w, ada_b)
    bias = _attn_bias()
    n_route = N_GROUPS + N_EXPERTS
    y_prompt = y_sample = None
    for l in range(DEPTH):
        modr = mod[l, :ns].reshape(ns * 6, 1, D_MODEL)
        qkv, u = _proj_in(x, modr, norm1_g[l], w_in[l].astype(BF16), lay, tm_in)
        attn = _attention(qkv, bias, lay, group=0)
        attn = _attention(qkv, bias, lay, attn, group=1)
        convn = _conv(u, conv_w[l], conv_b[l], conv_ln_g[l], conv_ln_b[l], conv_out_g[l], lay, ts_conv)
        wr = jnp.zeros((D_MODEL, ROUTE_LANES), F32).at[:, :n_route].set(
            jnp.concatenate([rg_w[l], re_w[l]], axis=1)).astype(BF16)
        br = jnp.zeros((1, ROUTE_LANES), F32).at[0, :n_route].set(jnp.concatenate([rg_b[l], re_b[l]]))
        x_mid, h2, ri, rf = _proj_out(attn, convn, x, modr, attn_out_g[l], w_out[l].astype(BF16),
                                      norm2_g[l], wr, br, lay, tm_out)
        row_tok, pos, block_expert, n_valid = _dispatch(ri[:, :TOP_K], tm_moe)
        xs = h2.at[row_tok].get(mode='promise_in_bounds')
        act = _gate_up(xs, e_gate, e_up, block_expert, n_valid, l, tm_moe)
        ys = _down(act, e_down, block_expert, n_valid, l, tm_moe)
        y0 = ys.at[pos[:, 0]].get(mode='promise_in_bounds')
        y1 = ys.at[pos[:, 1]].get(mode='promise_in_bounds')
        if l < DEPTH - 1:
            x = _combine(x_mid, y0, y1, rf, modr, lay, tm_res)
        else:
            y_prompt, y_sample = _final(x_mid, y0, y1, rf, modr, final_g, lay, tm_res)
    return (y_prompt.reshape(batch, seq, D_MODEL), y_sample.reshape(dec_batch, dec_seq, D_MODEL))


def kernel(x_prompt, x_sample, c_prompt, c_sample, ada_w, ada_b, norm1_g, w_in, conv_w, conv_b, conv_ln_g, conv_ln_b, attn_out_g, conv_out_g, w_out, norm2_g, rg_w, rg_b, re_w, re_b, e_gate, e_up, e_down, final_g):
    batch, seq, _ = x_prompt.shape
    dec_batch, dec_seq, _ = x_sample.shape
    lay = (batch, seq, dec_batch, dec_seq)
    x = jnp.concatenate([x_prompt.reshape(batch * seq, D_MODEL),
                         x_sample.reshape(dec_batch * dec_seq, D_MODEL)], axis=0)
    c_all = jnp.concatenate([c_prompt, c_sample], axis=0)
    return _trunk(x, c_all, lay, ada_w, ada_b, norm1_g, w_in, conv_w, conv_b, conv_ln_g, conv_ln_b,
                  attn_out_g, conv_out_g, w_out, norm2_g, rg_w, rg_b, re_w, re_b,
                  e_gate, e_up, e_down, final_g)
```

```python
import functools
import math

import jax
import jax.numpy as jnp
from jax import lax
from jax.experimental import pallas as pl
from jax.experimental.pallas import tpu as pltpu

D_MODEL = 2048
DEPTH = 4
N_HEADS = 8
HEAD_DIM = 128
ATTN_WIDTH = N_HEADS * HEAD_DIM
CONV_CH = D_MODEL - ATTN_WIDTH
CONV_WIDTH = 31
DILATED_CONFIGS = ((128, 1), (512, 4), (2048, 16))
N_GROUPS = 4
EXPERTS_PER_GROUP = 8
N_EXPERTS = N_GROUPS * EXPERTS_PER_GROUP
TOP_K = 2
D_EXPERT = 1024
NORM_EPS = 1e-6
MASK_VALUE = -1e30

LANES = 128
ATT_TQ = 128
ATT_TK = 256
ATT_HALF = 64
ATT_STAGE = 256
ATT_UNROLL = 8
CONV_HALO = 16
ROUTE_LANES = 128
VMEM_LIMIT = 56 * 1024 * 1024

F32 = jnp.float32
BF16 = jnp.bfloat16


def _params(sem, vmem=VMEM_LIMIT):
    return pltpu.CompilerParams(dimension_semantics=sem, vmem_limit_bytes=vmem)


def _seq_of_row(row, lay):
    batch, seq, _, dec_seq = lay
    n_prompt = batch * seq
    return jnp.where(row < n_prompt, row // seq, batch + (row - n_prompt) // dec_seq)


def _ada_body(c_ref, w_ref, b_ref, o_ref):
    c = c_ref[...]
    sc = c * jax.nn.sigmoid(c)
    o_ref[0] = jnp.dot(sc, w_ref[0], preferred_element_type=F32) + b_ref[0]


def _ada_mod(c_pad, ada_w, ada_b):
    ns8 = c_pad.shape[0]
    tn = 1024
    nj = 6 * D_MODEL // tn
    return pl.pallas_call(
        _ada_body,
        grid=(DEPTH, nj),
        in_specs=[
            pl.BlockSpec((ns8, D_MODEL), lambda l, j: (0, 0)),
            pl.BlockSpec((1, D_MODEL, tn), lambda l, j: (l, 0, j)),
            pl.BlockSpec((1, 1, tn), lambda l, j: (l, 0, j)),
        ],
        out_specs=pl.BlockSpec((1, ns8, tn), lambda l, j: (l, 0, j)),
        out_shape=jax.ShapeDtypeStruct((DEPTH, ns8, 6 * D_MODEL), F32),
        compiler_params=_params(("arbitrary", "arbitrary")),
    )(c_pad, ada_w, ada_b.reshape(DEPTH, 1, 6 * D_MODEL))


def _proj_in_body(x_ref, shift_ref, scale_ref, g_ref, wqkv_ref, wga_ref, wgb_ref,
                  qkv_ref, u_ref, h_scr):
    j = pl.program_id(1)

    @pl.when(j == 0)
    def _():
        x = x_ref[...]
        ms = jnp.mean(x * x, axis=-1, keepdims=True)
        y = x * lax.rsqrt(ms + NORM_EPS) * g_ref[...]
        h_scr[...] = (y * (1.0 + scale_ref[0]) + shift_ref[0]).astype(BF16)

    @pl.when(j < 3)
    def _():
        r = jnp.dot(h_scr[...], wqkv_ref[...], preferred_element_type=F32)
        r = r * jnp.where(j == 0, HEAD_DIM ** -0.5, 1.0).astype(F32)
        for hh in range(N_HEADS):
            qkv_ref[hh] = r[:, hh * HEAD_DIM:(hh + 1) * HEAD_DIM]

    @pl.when(j >= 3)
    def _():
        h = h_scr[...]
        a = jnp.dot(h, wga_ref[...], preferred_element_type=F32)
        b = jnp.dot(h, wgb_ref[...], preferred_element_type=F32)
        u_ref[...] = a * jax.nn.sigmoid(b)


def _proj_in(x, modr, norm_g, w_in_bf, lay, tm):
    t = x.shape[0]
    tn = ATTN_WIDTH
    tg = CONV_CH // 2
    ng = CONV_CH // tg
    nq = 3 * ATTN_WIDTH // tg

    def seq(i):
        return _seq_of_row(i * tm, lay)

    def gcol(j):
        return jnp.clip(j - 3, 0, ng - 1)

    return pl.pallas_call(
        _proj_in_body,
        grid=(t // tm, 3 + ng),
        in_specs=[
            pl.BlockSpec((tm, D_MODEL), lambda i, j: (i, 0)),
            pl.BlockSpec((1, 1, D_MODEL), lambda i, j: (seq(i) * 6 + 0, 0, 0)),
            pl.BlockSpec((1, 1, D_MODEL), lambda i, j: (seq(i) * 6 + 1, 0, 0)),
            pl.BlockSpec((1, D_MODEL), lambda i, j: (0, 0)),
            pl.BlockSpec((D_MODEL, tn), lambda i, j: (0, jnp.minimum(j, 2))),
            pl.BlockSpec((D_MODEL, tg), lambda i, j: (0, nq + gcol(j))),
            pl.BlockSpec((D_MODEL, tg), lambda i, j: (0, nq + ng + gcol(j))),
        ],
        out_specs=[
            pl.BlockSpec((N_HEADS, tm, HEAD_DIM), lambda i, j: (jnp.minimum(j, 2), i, 0)),
            pl.BlockSpec((tm, tg), lambda i, j: (i, gcol(j))),
        ],
        out_shape=[
            jax.ShapeDtypeStruct((3 * N_HEADS, t, HEAD_DIM), F32),
            jax.ShapeDtypeStruct((t, CONV_CH), F32),
        ],
        scratch_shapes=[pltpu.VMEM((tm, D_MODEL), BF16)],
        compiler_params=_params(("arbitrary", "arbitrary")),
    )(x, modr, modr, norm_g.reshape(1, D_MODEL), w_in_bf, w_in_bf, w_in_bf)


def _alibi_slopes():
    return jnp.exp2(-8.0 * jnp.arange(1, N_HEADS + 1, dtype=F32) / N_HEADS)


def _attn_bias():
    i = jnp.arange(ATT_TQ)[:, None]
    j = jnp.arange(ATT_TK)[None, :]
    out = []
    for _, d in DILATED_CONFIGS:
        for v in range(3):
            rel = jnp.abs(j - ATT_HALF * v - i)
            dist = (d * rel).astype(F32)
            b = -_alibi_slopes()[:, None, None] * dist[None]
            out.append(jnp.where((rel <= ATT_HALF)[None], b, MASK_VALUE))
    return jnp.stack(out, axis=1)


def _strided_rows(start, size, stride):
    return pl.ds(start, size, stride=stride) if stride > 1 else pl.ds(start, size)


def _attn_body(q_ref, k_ref, v_ref, bias_ref, *rest, seq_len):
    o_ref, lse_scr, qs, ks, vs = rest[-5:]
    order = sorted(range(len(DILATED_CONFIGS)), key=lambda c: -DILATED_CONFIGS[c][1])
    for step, c in enumerate(order):
        d = DILATED_CONFIGS[c][1]
        sub_len = seq_len // d
        first, last = step == 0, step == len(order) - 1

        nstage = sub_len // ATT_STAGE
        nt = sub_len // ATT_TQ

        def stage(n, cr, d=d, sub_len=sub_len, nstage=nstage):
            r = lax.div(n, nstage)
            ch = lax.rem(n, nstage)
            dst = pl.ds(pl.multiple_of(n * ATT_STAGE, ATT_STAGE), ATT_STAGE)
            src = _strided_rows(r + ch * (ATT_STAGE * d), ATT_STAGE, d)
            qs[dst, :] = q_ref[0, src, :].astype(BF16)
            ks[dst, :] = k_ref[0, src, :].astype(BF16)
            vs[dst, :] = v_ref[0, src, :].astype(BF16)
            return cr

        lax.fori_loop(0, d * nstage, stage, 0, unroll=2)

        def tile(n, ct, c=c, d=d, sub_len=sub_len, nt=nt, first=first, last=last):
            r = lax.div(n, nt)
            l0 = lax.rem(n, nt) * ATT_TQ
            k0 = jnp.clip(l0 - ATT_HALF, 0, sub_len - ATT_TK)
            var = lax.div(l0 - k0, ATT_HALF)
            base = r * sub_len
            q = qs[pl.ds(pl.multiple_of(base + l0, ATT_TQ), ATT_TQ), :]
            k = ks[pl.ds(pl.multiple_of(base + k0, ATT_HALF), ATT_TK), :]
            v = vs[pl.ds(pl.multiple_of(base + k0, ATT_HALF), ATT_TK), :]
            s = lax.dot_general(q, k, (((1,), (1,)), ((), ())), preferred_element_type=F32)
            s = s + bias_ref[0, 3 * c + var]
            m = jnp.max(s, axis=-1, keepdims=True)
            p = jnp.exp(s - m)
            l = jnp.sum(p, axis=-1, keepdims=True)
            o = jnp.dot(p.astype(BF16), v, preferred_element_type=F32)
            mb = jnp.broadcast_to(m, (ATT_TQ, LANES))
            lb = jnp.broadcast_to(l, (ATT_TQ, LANES))
            rows = _strided_rows(r + l0 * d, ATT_TQ, d)
            if first:
                o_ref[rows, :] = o / lb
                lse_scr[rows, :] = mb + jnp.log(lb)
            else:
                la = lse_scr[rows, :]
                mx = jnp.maximum(la, mb)
                wa = jnp.exp(la - mx)
                wb = jnp.exp(mb - mx)
                den = wa + wb * lb
                o_ref[rows, :] = (wa * o_ref[rows, :] + wb * o) / den
                if not last:
                    lse_scr[rows, :] = mx + jnp.log(den)
            return ct

        lax.fori_loop(0, d * nt, tile, 0, unroll=ATT_UNROLL)


def _attention(qkv, bias, lay, *, group):
    batch, seq, dec_batch, dec_seq = lay
    if group == 0:
        nb, s_len, blk0 = batch, seq, 0
    else:
        assert (batch * seq) % dec_seq == 0
        nb, s_len, blk0 = dec_batch, dec_seq, batch * seq // dec_seq
    assert s_len % (ATT_TK * DILATED_CONFIGS[-1][1]) == 0

    def hm(part):
        return lambda b, h: (part * N_HEADS + h, blk0 + b, 0)

    return pl.pallas_call(
        functools.partial(_attn_body, seq_len=s_len),
        grid=(nb, N_HEADS),
        in_specs=[
            pl.BlockSpec((1, s_len, HEAD_DIM), hm(0)),
            pl.BlockSpec((1, s_len, HEAD_DIM), hm(1)),
            pl.BlockSpec((1, s_len, HEAD_DIM), hm(2)),
            pl.BlockSpec((1, bias.shape[1], ATT_TQ, ATT_TK), lambda b, h: (h, 0, 0, 0)),
        ],
        out_specs=pl.BlockSpec((s_len, HEAD_DIM), lambda b, h: (b, h)),
        out_shape=jax.ShapeDtypeStruct((nb * s_len, ATTN_WIDTH), F32),
        scratch_shapes=[pltpu.VMEM((s_len, LANES), F32)] + [pltpu.VMEM((s_len, HEAD_DIM), BF16)] * 3,
        compiler_params=_params(("arbitrary", "arbitrary")),
    )(qkv, qkv, qkv, bias)


def _conv_body(prev_ref, cur_ref, next_ref, w_ref, cb_ref, lg_ref, lb_ref, og_ref, o_ref,
               pad_scr, cv_scr, *, lay, ts):
    batch, seq, dec_batch, dec_seq = lay
    n_prompt = batch * seq
    row0 = pl.program_id(0) * ts
    row1 = row0 + ts
    pos0 = jnp.where(row0 < n_prompt, row0 % seq, (row0 - n_prompt) % dec_seq)
    pos1 = jnp.where(row1 <= n_prompt, row1 % seq, (row1 - n_prompt) % dec_seq)
    nch = CONV_CH // LANES
    lanes = [slice(cc * LANES, (cc + 1) * LANES) for cc in range(nch)]
    for cc in range(nch):
        pad_scr[cc, 0:CONV_HALO, :] = jnp.where(pos0 != 0, prev_ref[:, lanes[cc]], 0.0)
        pad_scr[cc, CONV_HALO:CONV_HALO + ts, :] = cur_ref[:, lanes[cc]]
        pad_scr[cc, CONV_HALO + ts:, :] = jnp.where(pos1 != 0, next_ref[:, lanes[cc]], 0.0)

    rc = 128
    base = CONV_HALO - CONV_WIDTH // 2

    def lane_chunk(cc, carry):
        for r in range(ts // rc):
            for par in range(2):
                acc = jnp.zeros((rc // 2, LANES), F32)
                for j in range(CONV_WIDTH):
                    acc = acc + (pad_scr[cc, pl.ds(base + r * rc + par + j, rc // 2, stride=2), :]
                                 * w_ref[cc, pl.ds(j, 1), :])
                cv_scr[cc, pl.ds(r * rc + par, rc // 2, stride=2), :] = acc
        return carry

    lax.fori_loop(0, nch, lane_chunk, 0)

    rn = 32
    inv_c = 1.0 / CONV_CH

    def row_chunk(r, carry):
        r0 = pl.multiple_of(r * rn, rn)
        u = [cv_scr[cc, pl.ds(r0, rn), :] + cb_ref[:, lanes[cc]] for cc in range(nch)]
        mu = jnp.sum(sum(u), axis=-1, keepdims=True) * inv_c
        u = [a - mu for a in u]
        var = jnp.sum(sum(a * a for a in u), axis=-1, keepdims=True) * inv_c
        rstd = lax.rsqrt(var + NORM_EPS)
        y = [a * rstd * lg_ref[:, lanes[cc]] + lb_ref[:, lanes[cc]] for cc, a in enumerate(u)]
        y = [a * jax.nn.sigmoid(a) for a in y]
        ms = jnp.sum(sum(a * a for a in y), axis=-1, keepdims=True) * inv_c
        rr = lax.rsqrt(ms + NORM_EPS)
        for cc, a in enumerate(y):
            o_ref[pl.ds(r0, rn), lanes[cc]] = (a * rr * og_ref[:, lanes[cc]]).astype(o_ref.dtype)
        return carry

    lax.fori_loop(0, ts // rn, row_chunk, 0, unroll=4)


def _conv(u, conv_w, conv_b, ln_g, ln_b, out_g, lay, ts):
    t = u.shape[0]
    hb = ts // CONV_HALO
    nhb = t // CONV_HALO
    nch = CONV_CH // LANES
    w_slabs = conv_w.reshape(CONV_WIDTH, nch, LANES).transpose(1, 0, 2)
    row = lambda a: a.reshape(1, CONV_CH)
    return pl.pallas_call(
        functools.partial(_conv_body, lay=lay, ts=ts),
        grid=(t // ts,),
        in_specs=[
            pl.BlockSpec((CONV_HALO, CONV_CH), lambda i: (jnp.maximum(i * hb - 1, 0), 0)),
            pl.BlockSpec((ts, CONV_CH), lambda i: (i, 0)),
            pl.BlockSpec((CONV_HALO, CONV_CH), lambda i: (jnp.minimum((i + 1) * hb, nhb - 1), 0)),
            pl.BlockSpec((nch, CONV_WIDTH, LANES), lambda i: (0, 0, 0)),
            pl.BlockSpec((1, CONV_CH), lambda i: (0, 0)),
            pl.BlockSpec((1, CONV_CH), lambda i: (0, 0)),
            pl.BlockSpec((1, CONV_CH), lambda i: (0, 0)),
            pl.BlockSpec((1, CONV_CH), lambda i: (0, 0)),
        ],
        out_specs=pl.BlockSpec((ts, CONV_CH), lambda i: (i, 0)),
        out_shape=jax.ShapeDtypeStruct((t, CONV_CH), BF16),
        scratch_shapes=[pltpu.VMEM((nch, ts + 2 * CONV_HALO, LANES), F32),
                        pltpu.VMEM((nch, ts, LANES), F32)],
        compiler_params=_params(("arbitrary",)),
    )(u, u, u, w_slabs, row(conv_b), row(ln_g), row(ln_b), row(out_g))


def _proj_out_body(attn_p_ref, attn_s_ref, conv_ref, x_ref, ag_ref, wout_ref, gate_ref, g2_ref, shift_ref,
                   scale_ref, wr_ref, br_ref, xmid_ref, h_ref, ri_ref, rf_ref, *, n_prompt_blocks):
    a = jnp.where(pl.program_id(0) < n_prompt_blocks, attn_p_ref[...], attn_s_ref[...])
    ms = jnp.mean(a * a, axis=-1, keepdims=True)
    an = (a * lax.rsqrt(ms + NORM_EPS) * ag_ref[...]).astype(BF16)
    o = jnp.dot(an, wout_ref[0:ATTN_WIDTH, :], preferred_element_type=F32)
    o = o + jnp.dot(conv_ref[...], wout_ref[ATTN_WIDTH:, :], preferred_element_type=F32)
    x = x_ref[...] + gate_ref[0] * o
    xmid_ref[...] = x
    ms2 = jnp.mean(x * x, axis=-1, keepdims=True)
    y = x * lax.rsqrt(ms2 + NORM_EPS) * g2_ref[...]
    hb = (y * (1.0 + scale_ref[0]) + shift_ref[0]).astype(BF16)
    bits = pltpu.bitcast(hb.astype(F32), jnp.uint32)
    half = D_MODEL // 2
    h_ref[...] = (bits[:, :half] >> 16) | (bits[:, half:] & jnp.uint32(0xFFFF0000))

    lg = jnp.dot(hb, wr_ref[...], preferred_element_type=F32) + br_ref[...]
    lane = lax.broadcasted_iota(jnp.int32, lg.shape, 1)
    lane_f = lane.astype(F32)
    neg = jnp.float32(-jnp.inf)
    big = jnp.float32(ROUTE_LANES)
    gmask = lane < N_GROUPS
    lgm = jnp.where(gmask, lg, neg)
    gmax = jnp.max(lgm, axis=-1, keepdims=True)
    gidx = jnp.min(jnp.where(lgm == gmax, lane_f, big), axis=-1, keepdims=True).astype(jnp.int32)
    gsum = jnp.sum(jnp.where(gmask, jnp.exp(lg - gmax), 0.0), axis=-1, keepdims=True)
    pg = 1.0 / gsum
    lo = N_GROUPS + gidx * EXPERTS_PER_GROUP
    emask = (lane >= lo) & (lane < lo + EXPERTS_PER_GROUP)
    le1 = jnp.where(emask, lg, neg)
    e1 = jnp.max(le1, axis=-1, keepdims=True)
    i1 = jnp.min(jnp.where(le1 == e1, lane_f, big), axis=-1, keepdims=True)
    le2 = jnp.where(lane_f == i1, neg, le1)
    e2 = jnp.max(le2, axis=-1, keepdims=True)
    i2 = jnp.min(jnp.where(le2 == e2, lane_f, big), axis=-1, keepdims=True)
    r = jnp.exp(e2 - e1)
    w1 = pg / (1.0 + r)
    w2 = pg * r / (1.0 + r)
    id1 = (i1 - N_GROUPS).astype(jnp.int32)
    id2 = (i2 - N_GROUPS).astype(jnp.int32)
    ri_ref[...] = jnp.where(lane == 0, id1, jnp.where(lane == 1, id2, 0))
    rf_ref[...] = jnp.where(lane == 0, w1, jnp.where(lane == 1, w2, 0.0))


def _proj_out(attn_p, attn_s, convn, x, modr, attn_g, w_out_bf, norm2_g, wr, br, lay, tm):
    t = x.shape[0]
    npb = attn_p.shape[0] // tm

    def seq(i):
        return _seq_of_row(i * tm, lay)

    row = lambda a, n: a.reshape(1, n)
    return pl.pallas_call(
        functools.partial(_proj_out_body, n_prompt_blocks=npb),
        grid=(t // tm,),
        in_specs=[
            pl.BlockSpec((tm, ATTN_WIDTH), lambda i: (jnp.minimum(i, npb - 1), 0)),
            pl.BlockSpec((tm, ATTN_WIDTH), lambda i: (jnp.maximum(i - npb, 0), 0)),
            pl.BlockSpec((tm, CONV_CH), lambda i: (i, 0)),
            pl.BlockSpec((tm, D_MODEL), lambda i: (i, 0)),
            pl.BlockSpec((1, ATTN_WIDTH), lambda i: (0, 0)),
            pl.BlockSpec((D_MODEL, D_MODEL), lambda i: (0, 0)),
            pl.BlockSpec((1, 1, D_MODEL), lambda i: (seq(i) * 6 + 2, 0, 0)),
            pl.BlockSpec((1, D_MODEL), lambda i: (0, 0)),
            pl.BlockSpec((1, 1, D_MODEL), lambda i: (seq(i) * 6 + 3, 0, 0)),
            pl.BlockSpec((1, 1, D_MODEL), lambda i: (seq(i) * 6 + 4, 0, 0)),
            pl.BlockSpec((D_MODEL, ROUTE_LANES), lambda i: (0, 0)),
            pl.BlockSpec((1, ROUTE_LANES), lambda i: (0, 0)),
        ],
        out_specs=[
            pl.BlockSpec((tm, D_MODEL), lambda i: (i, 0)),
            pl.BlockSpec((tm, D_MODEL // 2), lambda i: (i, 0)),
            pl.BlockSpec((tm, ROUTE_LANES), lambda i: (i, 0)),
            pl.BlockSpec((tm, ROUTE_LANES), lambda i: (i, 0)),
        ],
        out_shape=[
            jax.ShapeDtypeStruct((t, D_MODEL), F32),
            jax.ShapeDtypeStruct((t, D_MODEL // 2), jnp.uint32),
            jax.ShapeDtypeStruct((t, ROUTE_LANES), jnp.int32),
            jax.ShapeDtypeStruct((t, ROUTE_LANES), F32),
        ],
        compiler_params=_params(("arbitrary",)),
    )(attn_p, attn_s, convn, x, row(attn_g, ATTN_WIDTH), w_out_bf, modr, row(norm2_g, D_MODEL), modr, modr,
      wr, br)


def _dispatch(experts, tm):
    t = experts.shape[0]
    a = t * TOP_K
    nb = -(-(a + N_EXPERTS * (tm - 1)) // tm)
    flat_e = experts.reshape(-1)
    order = jnp.argsort(flat_e).astype(jnp.int32)
    rank = jnp.argsort(order).astype(jnp.int32)
    counts = jnp.sum(flat_e[:, None] == jnp.arange(N_EXPERTS)[None, :], axis=0).astype(jnp.int32)
    starts = jnp.cumsum(counts) - counts
    padded = ((counts + tm - 1) // tm) * tm
    padded_ends = jnp.cumsum(padded)
    padded_starts = padded_ends - padded
    shift = (padded_starts - starts).astype(jnp.int32)
    pos = (rank + shift[flat_e]).reshape(t, TOP_K)
    block_expert = jnp.minimum(
        jnp.sum(padded_ends[None, :] <= (jnp.arange(nb) * tm)[:, None], axis=1), N_EXPERTS - 1).astype(jnp.int32)
    row = jnp.arange(nb * tm, dtype=jnp.int32)
    row_e = jnp.repeat(block_expert, tm)
    sorted_pos = row - shift[row_e]
    valid = sorted_pos < (starts + counts)[row_e]
    src = order[jnp.clip(sorted_pos, 0, a - 1)]
    row_tok = jnp.where(valid, src // TOP_K, row % t).astype(jnp.int32)
    n_valid = (padded_ends[-1] // tm).astype(jnp.int32).reshape(1)
    nonempty = counts > 0
    slot = ((jnp.cumsum(nonempty) - nonempty) % 2).astype(jnp.int32)
    ids = jnp.where(nonempty, jnp.arange(N_EXPERTS), N_EXPERTS)
    nxt = lax.cummin(jnp.concatenate([ids[1:], jnp.array([N_EXPERTS])]), reverse=True)
    nxt = jnp.where(nxt >= N_EXPERTS, -1, nxt).astype(jnp.int32)
    run_info = jnp.concatenate([slot, nxt, (padded_starts // tm).astype(jnp.int32)])
    return row_tok, pos, block_expert, n_valid, run_info


def _unpack_rows(words):
    lo = pltpu.bitcast(words << 16, F32).astype(BF16)
    hi = pltpu.bitcast(words & jnp.uint32(0xFFFF0000), F32).astype(BF16)
    return lo, hi


def _expert_weights(be_ref, ri_ref, hbm_refs, bufs, sem, layer, tn):
    j, i = pl.program_id(0), pl.program_id(1)
    e = be_ref[i]
    slot, nxt, first_blk = ri_ref[e], ri_ref[N_EXPERTS + e], ri_ref[2 * N_EXPERTS + e]

    def copies(expert, s):
        return [pltpu.make_async_copy(h.at[layer, expert, :, pl.ds(pl.multiple_of(j * tn, tn), tn)], b.at[s],
                                      sem.at[n, s])
                for n, (h, b) in enumerate(zip(hbm_refs, bufs))]

    @pl.when(i == first_blk)
    def _():
        @pl.when(i == 0)
        def _():
            for c in copies(e, slot):
                c.start()

        for c in copies(e, slot):
            c.wait()

        @pl.when(nxt >= 0)
        def _():
            for c in copies(nxt, 1 - slot):
                c.start()

    return slot


def _gate_up_body(be_ref, nv_ref, ri_ref, xs_ref, wg_hbm, wu_hbm, a_ref, wg_buf, wu_buf, sem, *, layer, tn):
    i = pl.program_id(1)
    half = D_MODEL // 2

    @pl.when(i < nv_ref[0])
    def _():
        slot = _expert_weights(be_ref, ri_ref, (wg_hbm, wu_hbm), (wg_buf, wu_buf), sem, layer, tn)
        lo, hi = _unpack_rows(xs_ref[...])
        g = (jnp.dot(lo, wg_buf[slot, :half, :].astype(BF16), preferred_element_type=F32)
             + jnp.dot(hi, wg_buf[slot, half:, :].astype(BF16), preferred_element_type=F32))
        u = (jnp.dot(lo, wu_buf[slot, :half, :].astype(BF16), preferred_element_type=F32)
             + jnp.dot(hi, wu_buf[slot, half:, :].astype(BF16), preferred_element_type=F32))
        a_ref[...] = (g * jax.nn.sigmoid(g) * u).astype(a_ref.dtype)

    @pl.when(i >= nv_ref[0])
    def _():
        a_ref[...] = jnp.zeros_like(a_ref)


def _gate_up(xs, e_gate, e_up, block_expert, n_valid, run_info, layer, tm):
    p = xs.shape[0]
    tn = D_EXPERT // 2
    return pl.pallas_call(
        functools.partial(_gate_up_body, layer=layer, tn=tn),
        grid_spec=pltpu.PrefetchScalarGridSpec(
            num_scalar_prefetch=3,
            grid=(D_EXPERT // tn, p // tm),
            in_specs=[
                pl.BlockSpec((tm, D_MODEL // 2), lambda j, i, be, nv, ri: (jnp.minimum(i, nv[0] - 1), 0)),
                pl.BlockSpec(memory_space=pl.ANY),
                pl.BlockSpec(memory_space=pl.ANY),
            ],
            out_specs=pl.BlockSpec((tm, tn), lambda j, i, be, nv, ri: (i, j)),
            scratch_shapes=[pltpu.VMEM((2, D_MODEL, tn), F32), pltpu.VMEM((2, D_MODEL, tn), F32),
                            pltpu.SemaphoreType.DMA((2, 2))],
        ),
        out_shape=jax.ShapeDtypeStruct((p, D_EXPERT), BF16),
        compiler_params=_params(("arbitrary", "arbitrary")),
    )(block_expert, n_valid, run_info, xs, e_gate, e_up)


def _down_body(be_ref, nv_ref, ri_ref, a_ref, wd_hbm, y_ref, wd_buf, sem, *, layer, tn):
    i = pl.program_id(1)

    @pl.when(i < nv_ref[0])
    def _():
        slot = _expert_weights(be_ref, ri_ref, (wd_hbm,), (wd_buf,), sem, layer, tn)
        y_ref[...] = jnp.dot(a_ref[...], wd_buf[slot].astype(BF16), preferred_element_type=F32)

    @pl.when(i >= nv_ref[0])
    def _():
        y_ref[...] = jnp.zeros_like(y_ref)


def _down(a, e_down, block_expert, n_valid, run_info, layer, tm):
    p = a.shape[0]
    tn = D_MODEL // 2
    return pl.pallas_call(
        functools.partial(_down_body, layer=layer, tn=tn),
        grid_spec=pltpu.PrefetchScalarGridSpec(
            num_scalar_prefetch=3,
            grid=(D_MODEL // tn, p // tm),
            in_specs=[
                pl.BlockSpec((tm, D_EXPERT), lambda j, i, be, nv, ri: (jnp.minimum(i, nv[0] - 1), 0)),
                pl.BlockSpec(memory_space=pl.ANY),
            ],
            out_specs=pl.BlockSpec((tm, tn), lambda j, i, be, nv, ri: (i, j)),
            scratch_shapes=[pltpu.VMEM((2, D_EXPERT, tn), F32), pltpu.SemaphoreType.DMA((1, 2))],
        ),
        out_shape=jax.ShapeDtypeStruct((p, D_MODEL), F32),
        compiler_params=_params(("arbitrary", "arbitrary")),
    )(block_expert, n_valid, run_info, a, e_down)


def _moe_residual(x_ref, y0_ref, y1_ref, rf_ref, gate_ref):
    rf = rf_ref[...]
    return x_ref[...] + gate_ref[0] * (rf[:, 0:1] * y0_ref[...] + rf[:, 1:2] * y1_ref[...])


def _combine_body(x_ref, y0_ref, y1_ref, rf_ref, gate_ref, o_ref):
    o_ref[...] = _moe_residual(x_ref, y0_ref, y1_ref, rf_ref, gate_ref)


def _combine(x, y0, y1, rf, modr, lay, tm):
    t = x.shape[0]
    blk = pl.BlockSpec((tm, D_MODEL), lambda i: (i, 0))
    return pl.pallas_call(
        _combine_body,
        grid=(t // tm,),
        in_specs=[blk, blk, blk, pl.BlockSpec((tm, ROUTE_LANES), lambda i: (i, 0)),
                  pl.BlockSpec((1, 1, D_MODEL), lambda i: (_seq_of_row(i * tm, lay) * 6 + 5, 0, 0))],
        out_specs=blk,
        out_shape=jax.ShapeDtypeStruct((t, D_MODEL), F32),
        compiler_params=_params(("arbitrary",)),
    )(x, y0, y1, rf, modr)


def _final_body(x_ref, y0_ref, y1_ref, rf_ref, gate_ref, g_ref, op_ref, os_ref, *, n_prompt_blocks):
    i = pl.program_id(0)
    x = _moe_residual(x_ref, y0_ref, y1_ref, rf_ref, gate_ref)
    ms = jnp.mean(x * x, axis=-1, keepdims=True)
    y = x * lax.rsqrt(ms + NORM_EPS) * g_ref[...]

    @pl.when(i < n_prompt_blocks)
    def _():
        op_ref[...] = y

    @pl.when(i >= n_prompt_blocks)
    def _():
        os_ref[...] = y


def _final(x, y0, y1, rf, modr, final_g, lay, tm):
    batch, seq, dec_batch, dec_seq = lay
    t = x.shape[0]
    n_prompt = batch * seq
    npb = n_prompt // tm
    blk = pl.BlockSpec((tm, D_MODEL), lambda i: (i, 0))
    return pl.pallas_call(
        functools.partial(_final_body, n_prompt_blocks=npb),
        grid=(t // tm,),
        in_specs=[blk, blk, blk, pl.BlockSpec((tm, ROUTE_LANES), lambda i: (i, 0)),
                  pl.BlockSpec((1, 1, D_MODEL), lambda i: (_seq_of_row(i * tm, lay) * 6 + 5, 0, 0)),
                  pl.BlockSpec((1, D_MODEL), lambda i: (0, 0))],
        out_specs=[
            pl.BlockSpec((tm, D_MODEL), lambda i: (jnp.minimum(i, npb - 1), 0)),
            pl.BlockSpec((tm, D_MODEL), lambda i: (jnp.maximum(i - npb, 0), 0)),
        ],
        out_shape=[
            jax.ShapeDtypeStruct((n_prompt, D_MODEL), F32),
            jax.ShapeDtypeStruct((t - n_prompt, D_MODEL), F32),
        ],
        compiler_params=_params(("arbitrary",)),
    )(x, y0, y1, rf, modr, final_g.reshape(1, D_MODEL))


def _trunk(x, c_all, lay, ada_w, ada_b, norm1_g, w_in, conv_w, conv_b, conv_ln_g, conv_ln_b,
           attn_out_g, conv_out_g, w_out, norm2_g, rg_w, rg_b, re_w, re_b, e_gate, e_up, e_down,
           final_g, *, tm_in=1024, tm_out=256, ts_conv=256, tm_moe=512, tm_res=512):
    batch, seq, dec_batch, dec_seq = lay
    t = x.shape[0]
    ns = c_all.shape[0]
    ns8 = -(-ns // 8) * 8
    c_pad = jnp.zeros((ns8, D_MODEL), F32).at[:ns].set(c_all)
    mod = _ada_mod(c_pad, ada_w, ada_b)
    bias = _attn_bias()
    n_route = N_GROUPS + N_EXPERTS
    y_prompt = y_sample = None
    for l in range(DEPTH):
        modr = mod[l, :ns].reshape(ns * 6, 1, D_MODEL)
        qkv, u = _proj_in(x, modr, norm1_g[l], w_in[l].astype(BF16), lay, tm_in)
        attn_p = _attention(qkv, bias, lay, group=0)
        attn_s = _attention(qkv, bias, lay, group=1)
        convn = _conv(u, conv_w[l], conv_b[l], conv_ln_g[l], conv_ln_b[l], conv_out_g[l], lay, ts_conv)
        wr = jnp.zeros((D_MODEL, ROUTE_LANES), F32).at[:, :n_route].set(
            jnp.concatenate([rg_w[l], re_w[l]], axis=1)).astype(BF16)
        br = jnp.zeros((1, ROUTE_LANES), F32).at[0, :n_route].set(jnp.concatenate([rg_b[l], re_b[l]]))
        x_mid, h2, ri, rf = _proj_out(attn_p, attn_s, convn, x, modr, attn_out_g[l], w_out[l].astype(BF16),
                                      norm2_g[l], wr, br, lay, tm_out)
        row_tok, pos, block_expert, n_valid, run_info = _dispatch(ri[:, :TOP_K], tm_moe)
        xs = h2.at[row_tok].get(mode='promise_in_bounds')
        act = _gate_up(xs, e_gate, e_up, block_expert, n_valid, run_info, l, tm_moe)
        ys = _down(act, e_down, block_expert, n_valid, run_info, l, tm_moe)
        y0 = ys.at[pos[:, 0]].get(mode='promise_in_bounds')
        y1 = ys.at[pos[:, 1]].get(mode='promise_in_bounds')
        if l < DEPTH - 1:
            x = _combine(x_mid, y0, y1, rf, modr, lay, tm_res)
        else:
            y_prompt, y_sample = _final(x_mid, y0, y1, rf, modr, final_g, lay, tm_res)
    return (y_prompt.reshape(batch, seq, D_MODEL), y_sample.reshape(dec_batch, dec_seq, D_MODEL))


def kernel(x_prompt, x_sample, c_prompt, c_sample, ada_w, ada_b, norm1_g, w_in, conv_w, conv_b, conv_ln_g, conv_ln_b, attn_out_g, conv_out_g, w_out, norm2_g, rg_w, rg_b, re_w, re_b, e_gate, e_up, e_down, final_g):
    batch, seq, _ = x_prompt.shape
    dec_batch, dec_seq, _ = x_sample.shape
    lay = (batch, seq, dec_batch, dec_seq)
    x = jnp.concatenate([x_prompt.reshape(batch * seq, D_MODEL),
                         x_sample.reshape(dec_batch * dec_seq, D_MODEL)], axis=0)
    c_all = jnp.concatenate([c_prompt, c_sample], axis=0)
    return _trunk(x, c_all, lay, ada_w, ada_b, norm1_g, w_in, conv_w, conv_b, conv_ln_g, conv_ln_b,
                  attn_out_g, conv_out_g, w_out, norm2_g, rg_w, rg_b, re_w, re_b,
                  e_gate, e_up, e_down, final_g)
```

```python
import functools
import math

import jax
import jax.numpy as jnp
from jax import lax
from jax.experimental import pallas as pl
from jax.experimental.pallas import tpu as pltpu

D_MODEL = 2048
DEPTH = 4
N_HEADS = 8
HEAD_DIM = 128
ATTN_WIDTH = N_HEADS * HEAD_DIM
CONV_CH = D_MODEL - ATTN_WIDTH
CONV_WIDTH = 31
DILATED_CONFIGS = ((128, 1), (512, 4), (2048, 16))
N_GROUPS = 4
EXPERTS_PER_GROUP = 8
N_EXPERTS = N_GROUPS * EXPERTS_PER_GROUP
TOP_K = 2
D_EXPERT = 1024
NORM_EPS = 1e-6
MASK_VALUE = -1e30

LANES = 128
ATT_TQ = 128
ATT_TK = 256
ATT_HALF = 64
ATT_STAGE = 256
ATT_UNROLL = 8
CONV_HALO = 16
ROUTE_LANES = 128
VMEM_LIMIT = 56 * 1024 * 1024

F32 = jnp.float32
BF16 = jnp.bfloat16


def _params(sem, vmem=VMEM_LIMIT):
    return pltpu.CompilerParams(dimension_semantics=sem, vmem_limit_bytes=vmem)


def _seq_of_row(row, lay):
    batch, seq, _, dec_seq = lay
    n_prompt = batch * seq
    return jnp.where(row < n_prompt, row // seq, batch + (row - n_prompt) // dec_seq)


def _ada_body(c_ref, w_ref, b_ref, o_ref):
    c = c_ref[...]
    sc = c * jax.nn.sigmoid(c)
    o_ref[0] = jnp.dot(sc, w_ref[0], preferred_element_type=F32) + b_ref[0]


def _ada_mod(c_pad, ada_w, ada_b):
    ns8 = c_pad.shape[0]
    tn = 1024
    nj = 6 * D_MODEL // tn
    return pl.pallas_call(
        _ada_body,
        grid=(DEPTH, nj),
        in_specs=[
            pl.BlockSpec((ns8, D_MODEL), lambda l, j: (0, 0)),
            pl.BlockSpec((1, D_MODEL, tn), lambda l, j: (l, 0, j)),
            pl.BlockSpec((1, 1, tn), lambda l, j: (l, 0, j)),
        ],
        out_specs=pl.BlockSpec((1, ns8, tn), lambda l, j: (l, 0, j)),
        out_shape=jax.ShapeDtypeStruct((DEPTH, ns8, 6 * D_MODEL), F32),
        compiler_params=_params(("arbitrary", "arbitrary")),
    )(c_pad, ada_w, ada_b.reshape(DEPTH, 1, 6 * D_MODEL))


def _norm_mod(x, g_ref, scale_ref, shift_ref):
    ms = jnp.mean(x * x, axis=-1, keepdims=True)
    y = x * lax.rsqrt(ms + NORM_EPS) * g_ref[...]
    return (y * (1.0 + scale_ref[0]) + shift_ref[0]).astype(BF16)


def _prologue_body(xp_ref, xs_ref, shift_ref, scale_ref, g_ref, x_ref, h_ref, *, n_prompt_blocks):
    x = jnp.where(pl.program_id(0) < n_prompt_blocks, xp_ref[...], xs_ref[...])
    x_ref[...] = x
    h_ref[...] = _norm_mod(x, g_ref, scale_ref, shift_ref)


def _prologue(x_prompt, x_sample, modr, norm_g, lay, tm):
    n_prompt, n_sample = x_prompt.shape[0], x_sample.shape[0]
    t = n_prompt + n_sample
    npb = n_prompt // tm
    blk = pl.BlockSpec((tm, D_MODEL), lambda i: (i, 0))
    return pl.pallas_call(
        functools.partial(_prologue_body, n_prompt_blocks=npb),
        grid=(t // tm,),
        in_specs=[
            pl.BlockSpec((tm, D_MODEL), lambda i: (jnp.minimum(i, npb - 1), 0)),
            pl.BlockSpec((tm, D_MODEL), lambda i: (jnp.maximum(i - npb, 0), 0)),
            pl.BlockSpec((1, 1, D_MODEL), lambda i: (_seq_of_row(i * tm, lay) * 6 + 0, 0, 0)),
            pl.BlockSpec((1, 1, D_MODEL), lambda i: (_seq_of_row(i * tm, lay) * 6 + 1, 0, 0)),
            pl.BlockSpec((1, D_MODEL), lambda i: (0, 0)),
        ],
        out_specs=[blk, blk],
        out_shape=[jax.ShapeDtypeStruct((t, D_MODEL), F32), jax.ShapeDtypeStruct((t, D_MODEL), BF16)],
        compiler_params=_params(("arbitrary",)),
    )(x_prompt, x_sample, modr, modr, norm_g.reshape(1, D_MODEL))


def _proj_in_body(h_ref, wqkv_ref, wga_ref, wgb_ref, qkv_ref, u_ref):
    j = pl.program_id(1)

    @pl.when(j < 3)
    def _():
        r = jnp.dot(h_ref[...], wqkv_ref[...], preferred_element_type=F32)
        r = r * jnp.where(j == 0, HEAD_DIM ** -0.5, 1.0).astype(F32)
        for hh in range(N_HEADS):
            qkv_ref[hh] = r[:, hh * HEAD_DIM:(hh + 1) * HEAD_DIM]

    @pl.when(j == 3)
    def _():
        h = h_ref[...]
        a = jnp.dot(h, wga_ref[...], preferred_element_type=F32)
        b = jnp.dot(h, wgb_ref[...], preferred_element_type=F32)
        u_ref[...] = a * jax.nn.sigmoid(b)


def _proj_in(h, w_in_bf, tm):
    t = h.shape[0]
    tn = ATTN_WIDTH
    assert CONV_CH == tn
    resident = dict(pipeline_mode=pl.Buffered(1))
    return pl.pallas_call(
        _proj_in_body,
        grid=(t // tm, 4),
        in_specs=[
            pl.BlockSpec((tm, D_MODEL), lambda i, j: (i, 0)),
            pl.BlockSpec((D_MODEL, tn), lambda i, j: (0, jnp.minimum(j, 2))),
            pl.BlockSpec((D_MODEL, tn), lambda i, j: (0, 3), **resident),
            pl.BlockSpec((D_MODEL, tn), lambda i, j: (0, 4), **resident),
        ],
        out_specs=[
            pl.BlockSpec((N_HEADS, tm, HEAD_DIM), lambda i, j: (jnp.minimum(j, 2), i, 0)),
            pl.BlockSpec((tm, CONV_CH), lambda i, j: (i, 0)),
        ],
        out_shape=[
            jax.ShapeDtypeStruct((3 * N_HEADS, t, HEAD_DIM), F32),
            jax.ShapeDtypeStruct((t, CONV_CH), F32),
        ],
        compiler_params=_params(("arbitrary", "arbitrary")),
    )(h, w_in_bf, w_in_bf, w_in_bf)


def _alibi_slopes():
    return jnp.exp2(-8.0 * jnp.arange(1, N_HEADS + 1, dtype=F32) / N_HEADS)


def _attn_bias():
    i = jnp.arange(ATT_TQ)[:, None]
    j = jnp.arange(ATT_TK)[None, :]
    out = []
    for _, d in DILATED_CONFIGS:
        for v in range(3):
            rel = jnp.abs(j - ATT_HALF * v - i)
            dist = (d * rel).astype(F32)
            b = -_alibi_slopes()[:, None, None] * dist[None]
            out.append(jnp.where((rel <= ATT_HALF)[None], b, MASK_VALUE))
    return jnp.stack(out, axis=1)


def _strided_rows(start, size, stride):
    return pl.ds(start, size, stride=stride) if stride > 1 else pl.ds(start, size)


def _attn_body(q_ref, k_ref, v_ref, bias_ref, *rest, seq_len):
    o_ref, lse_scr, qs, ks, vs = rest[-5:]
    order = sorted(range(len(DILATED_CONFIGS)), key=lambda c: -DILATED_CONFIGS[c][1])
    for step, c in enumerate(order):
        d = DILATED_CONFIGS[c][1]
        sub_len = seq_len // d
        first, last = step == 0, step == len(order) - 1

        nstage = sub_len // ATT_STAGE
        nt = sub_len // ATT_TQ

        def stage(n, cr, d=d, sub_len=sub_len, nstage=nstage):
            r = lax.div(n, nstage)
            ch = lax.rem(n, nstage)
            dst = pl.ds(pl.multiple_of(n * ATT_STAGE, ATT_STAGE), ATT_STAGE)
            src = _strided_rows(r + ch * (ATT_STAGE * d), ATT_STAGE, d)
            qs[dst, :] = q_ref[0, src, :].astype(BF16)
            ks[dst, :] = k_ref[0, src, :].astype(BF16)
            vs[dst, :] = v_ref[0, src, :].astype(BF16)
            return cr

        lax.fori_loop(0, d * nstage, stage, 0, unroll=2)

        def tile(n, ct, c=c, d=d, sub_len=sub_len, nt=nt, first=first, last=last):
            r = lax.div(n, nt)
            l0 = lax.rem(n, nt) * ATT_TQ
            k0 = jnp.clip(l0 - ATT_HALF, 0, sub_len - ATT_TK)
            var = lax.div(l0 - k0, ATT_HALF)
            base = r * sub_len
            q = qs[pl.ds(pl.multiple_of(base + l0, ATT_TQ), ATT_TQ), :]
            k = ks[pl.ds(pl.multiple_of(base + k0, ATT_HALF), ATT_TK), :]
            v = vs[pl.ds(pl.multiple_of(base + k0, ATT_HALF), ATT_TK), :]
            s = lax.dot_general(q, k, (((1,), (1,)), ((), ())), preferred_element_type=F32)
            s = s + bias_ref[0, 3 * c + var]
            m = jnp.max(s, axis=-1, keepdims=True)
            p = jnp.exp(s - m)
            l = jnp.sum(p, axis=-1, keepdims=True)
            o = jnp.dot(p.astype(BF16), v, preferred_element_type=F32)
            mb = jnp.broadcast_to(m, (ATT_TQ, LANES))
            lb = jnp.broadcast_to(l, (ATT_TQ, LANES))
            rows = _strided_rows(r + l0 * d, ATT_TQ, d)
            if first:
                o_ref[rows, :] = o / lb
                lse_scr[rows, :] = mb + jnp.log(lb)
            else:
                la = lse_scr[rows, :]
                mx = jnp.maximum(la, mb)
                wa = jnp.exp(la - mx)
                wb = jnp.exp(mb - mx)
                den = wa + wb * lb
                o_ref[rows, :] = (wa * o_ref[rows, :] + wb * o) / den
                if not last:
                    lse_scr[rows, :] = mx + jnp.log(den)
            return ct

        lax.fori_loop(0, d * nt, tile, 0, unroll=ATT_UNROLL)


def _attention(qkv, bias, lay, *, group):
    batch, seq, dec_batch, dec_seq = lay
    if group == 0:
        nb, s_len, blk0 = batch, seq, 0
    else:
        assert (batch * seq) % dec_seq == 0
        nb, s_len, blk0 = dec_batch, dec_seq, batch * seq // dec_seq
    assert s_len % (ATT_TK * DILATED_CONFIGS[-1][1]) == 0

    def hm(part):
        return lambda b, h: (part * N_HEADS + h, blk0 + b, 0)

    return pl.pallas_call(
        functools.partial(_attn_body, seq_len=s_len),
        grid=(nb, N_HEADS),
        in_specs=[
            pl.BlockSpec((1, s_len, HEAD_DIM), hm(0)),
            pl.BlockSpec((1, s_len, HEAD_DIM), hm(1)),
            pl.BlockSpec((1, s_len, HEAD_DIM), hm(2)),
            pl.BlockSpec((1, bias.shape[1], ATT_TQ, ATT_TK), lambda b, h: (h, 0, 0, 0)),
        ],
        out_specs=pl.BlockSpec((s_len, HEAD_DIM), lambda b, h: (b, h)),
        out_shape=jax.ShapeDtypeStruct((nb * s_len, ATTN_WIDTH), F32),
        scratch_shapes=[pltpu.VMEM((s_len, LANES), F32)] + [pltpu.VMEM((s_len, HEAD_DIM), BF16)] * 3,
        compiler_params=_params(("arbitrary", "arbitrary")),
    )(qkv, qkv, qkv, bias)


def _conv_body(prev_ref, cur_ref, next_ref, w_ref, cb_ref, lg_ref, lb_ref, og_ref, o_ref,
               pad_scr, cv_scr, *, lay, ts):
    batch, seq, dec_batch, dec_seq = lay
    n_prompt = batch * seq
    row0 = pl.program_id(0) * ts
    row1 = row0 + ts
    pos0 = jnp.where(row0 < n_prompt, row0 % seq, (row0 - n_prompt) % dec_seq)
    pos1 = jnp.where(row1 <= n_prompt, row1 % seq, (row1 - n_prompt) % dec_seq)
    nch = CONV_CH // LANES
    lanes = [slice(cc * LANES, (cc + 1) * LANES) for cc in range(nch)]
    for cc in range(nch):
        pad_scr[cc, 0:CONV_HALO, :] = jnp.where(pos0 != 0, prev_ref[:, lanes[cc]], 0.0)
        pad_scr[cc, CONV_HALO:CONV_HALO + ts, :] = cur_ref[:, lanes[cc]]
        pad_scr[cc, CONV_HALO + ts:, :] = jnp.where(pos1 != 0, next_ref[:, lanes[cc]], 0.0)

    rc = 128
    base = CONV_HALO - CONV_WIDTH // 2

    def lane_chunk(cc, carry):
        for r in range(ts // rc):
            for par in range(2):
                acc = jnp.zeros((rc // 2, LANES), F32)
                for j in range(CONV_WIDTH):
                    acc = acc + (pad_scr[cc, pl.ds(base + r * rc + par + j, rc // 2, stride=2), :]
                                 * w_ref[cc, pl.ds(j, 1), :])
                cv_scr[cc, pl.ds(r * rc + par, rc // 2, stride=2), :] = acc
        return carry

    lax.fori_loop(0, nch, lane_chunk, 0)

    rn = 32
    inv_c = 1.0 / CONV_CH

    def row_chunk(r, carry):
        r0 = pl.multiple_of(r * rn, rn)
        u = [cv_scr[cc, pl.ds(r0, rn), :] + cb_ref[:, lanes[cc]] for cc in range(nch)]
        mu = jnp.sum(sum(u), axis=-1, keepdims=True) * inv_c
        u = [a - mu for a in u]
        var = jnp.sum(sum(a * a for a in u), axis=-1, keepdims=True) * inv_c
        rstd = lax.rsqrt(var + NORM_EPS)
        y = [a * rstd * lg_ref[:, lanes[cc]] + lb_ref[:, lanes[cc]] for cc, a in enumerate(u)]
        y = [a * jax.nn.sigmoid(a) for a in y]
        ms = jnp.sum(sum(a * a for a in y), axis=-1, keepdims=True) * inv_c
        rr = lax.rsqrt(ms + NORM_EPS)
        for cc, a in enumerate(y):
            o_ref[pl.ds(r0, rn), lanes[cc]] = (a * rr * og_ref[:, lanes[cc]]).astype(o_ref.dtype)
        return carry

    lax.fori_loop(0, ts // rn, row_chunk, 0, unroll=4)


def _conv(u, conv_w, conv_b, ln_g, ln_b, out_g, lay, ts):
    t = u.shape[0]
    hb = ts // CONV_HALO
    nhb = t // CONV_HALO
    nch = CONV_CH // LANES
    w_slabs = conv_w.reshape(CONV_WIDTH, nch, LANES).transpose(1, 0, 2)
    row = lambda a: a.reshape(1, CONV_CH)
    return pl.pallas_call(
        functools.partial(_conv_body, lay=lay, ts=ts),
        grid=(t // ts,),
        in_specs=[
            pl.BlockSpec((CONV_HALO, CONV_CH), lambda i: (jnp.maximum(i * hb - 1, 0), 0)),
            pl.BlockSpec((ts, CONV_CH), lambda i: (i, 0)),
            pl.BlockSpec((CONV_HALO, CONV_CH), lambda i: (jnp.minimum((i + 1) * hb, nhb - 1), 0)),
            pl.BlockSpec((nch, CONV_WIDTH, LANES), lambda i: (0, 0, 0)),
            pl.BlockSpec((1, CONV_CH), lambda i: (0, 0)),
            pl.BlockSpec((1, CONV_CH), lambda i: (0, 0)),
            pl.BlockSpec((1, CONV_CH), lambda i: (0, 0)),
            pl.BlockSpec((1, CONV_CH), lambda i: (0, 0)),
        ],
        out_specs=pl.BlockSpec((ts, CONV_CH), lambda i: (i, 0)),
        out_shape=jax.ShapeDtypeStruct((t, CONV_CH), BF16),
        scratch_shapes=[pltpu.VMEM((nch, ts + 2 * CONV_HALO, LANES), F32),
                        pltpu.VMEM((nch, ts, LANES), F32)],
        compiler_params=_params(("arbitrary",)),
    )(u, u, u, w_slabs, row(conv_b), row(ln_g), row(ln_b), row(out_g))


def _proj_out_body(attn_p_ref, attn_s_ref, conv_ref, x_ref, ag_ref, wout_ref, gate_ref, g2_ref, shift_ref,
                   scale_ref, wr_ref, br_ref, xmid_ref, h_ref, ri_ref, rf_ref, *, n_prompt_blocks):
    a = jnp.where(pl.program_id(0) < n_prompt_blocks, attn_p_ref[...], attn_s_ref[...])
    ms = jnp.mean(a * a, axis=-1, keepdims=True)
    an = (a * lax.rsqrt(ms + NORM_EPS) * ag_ref[...]).astype(BF16)
    o = jnp.dot(an, wout_ref[0:ATTN_WIDTH, :], preferred_element_type=F32)
    o = o + jnp.dot(conv_ref[...], wout_ref[ATTN_WIDTH:, :], preferred_element_type=F32)
    x = x_ref[...] + gate_ref[0] * o
    xmid_ref[...] = x
    hb = _norm_mod(x, g2_ref, scale_ref, shift_ref)
    bits = pltpu.bitcast(hb.astype(F32), jnp.uint32)
    half = D_MODEL // 2
    h_ref[...] = (bits[:, :half] >> 16) | (bits[:, half:] & jnp.uint32(0xFFFF0000))

    lg = jnp.dot(hb, wr_ref[...], preferred_element_type=F32) + br_ref[...]
    lane = lax.broadcasted_iota(jnp.int32, lg.shape, 1)
    lane_f = lane.astype(F32)
    neg = jnp.float32(-jnp.inf)
    big = jnp.float32(ROUTE_LANES)
    gmask = lane < N_GROUPS
    lgm = jnp.where(gmask, lg, neg)
    gmax = jnp.max(lgm, axis=-1, keepdims=True)
    gidx = jnp.min(jnp.where(lgm == gmax, lane_f, big), axis=-1, keepdims=True).astype(jnp.int32)
    gsum = jnp.sum(jnp.where(gmask, jnp.exp(lg - gmax), 0.0), axis=-1, keepdims=True)
    pg = 1.0 / gsum
    lo = N_GROUPS + gidx * EXPERTS_PER_GROUP
    emask = (lane >= lo) & (lane < lo + EXPERTS_PER_GROUP)
    le1 = jnp.where(emask, lg, neg)
    e1 = jnp.max(le1, axis=-1, keepdims=True)
    i1 = jnp.min(jnp.where(le1 == e1, lane_f, big), axis=-1, keepdims=True)
    le2 = jnp.where(lane_f == i1, neg, le1)
    e2 = jnp.max(le2, axis=-1, keepdims=True)
    i2 = jnp.min(jnp.where(le2 == e2, lane_f, big), axis=-1, keepdims=True)
    r = jnp.exp(e2 - e1)
    w1 = pg / (1.0 + r)
    w2 = pg * r / (1.0 + r)
    id1 = (i1 - N_GROUPS).astype(jnp.int32)
    id2 = (i2 - N_GROUPS).astype(jnp.int32)
    ri_ref[...] = jnp.where(lane == 0, id1, jnp.where(lane == 1, id2, 0))
    rf_ref[...] = jnp.where(lane == 0, w1, jnp.where(lane == 1, w2, 0.0))


def _proj_out(attn_p, attn_s, convn, x, modr, attn_g, w_out_bf, norm2_g, wr, br, lay, tm):
    t = x.shape[0]
    npb = attn_p.shape[0] // tm

    def seq(i):
        return _seq_of_row(i * tm, lay)

    row = lambda a, n: a.reshape(1, n)
    return pl.pallas_call(
        functools.partial(_proj_out_body, n_prompt_blocks=npb),
        grid=(t // tm,),
        in_specs=[
            pl.BlockSpec((tm, ATTN_WIDTH), lambda i: (jnp.minimum(i, npb - 1), 0)),
            pl.BlockSpec((tm, ATTN_WIDTH), lambda i: (jnp.maximum(i - npb, 0), 0)),
            pl.BlockSpec((tm, CONV_CH), lambda i: (i, 0)),
            pl.BlockSpec((tm, D_MODEL), lambda i: (i, 0)),
            pl.BlockSpec((1, ATTN_WIDTH), lambda i: (0, 0)),
            pl.BlockSpec((D_MODEL, D_MODEL), lambda i: (0, 0), pipeline_mode=pl.Buffered(1)),
            pl.BlockSpec((1, 1, D_MODEL), lambda i: (seq(i) * 6 + 2, 0, 0)),
            pl.BlockSpec((1, D_MODEL), lambda i: (0, 0)),
            pl.BlockSpec((1, 1, D_MODEL), lambda i: (seq(i) * 6 + 3, 0, 0)),
            pl.BlockSpec((1, 1, D_MODEL), lambda i: (seq(i) * 6 + 4, 0, 0)),
            pl.BlockSpec((D_MODEL, ROUTE_LANES), lambda i: (0, 0)),
            pl.BlockSpec((1, ROUTE_LANES), lambda i: (0, 0)),
        ],
        out_specs=[
            pl.BlockSpec((tm, D_MODEL), lambda i: (i, 0)),
            pl.BlockSpec((tm, D_MODEL // 2), lambda i: (i, 0)),
            pl.BlockSpec((tm, ROUTE_LANES), lambda i: (i, 0)),
            pl.BlockSpec((tm, ROUTE_LANES), lambda i: (i, 0)),
        ],
        out_shape=[
            jax.ShapeDtypeStruct((t, D_MODEL), F32),
            jax.ShapeDtypeStruct((t, D_MODEL // 2), jnp.uint32),
            jax.ShapeDtypeStruct((t, ROUTE_LANES), jnp.int32),
            jax.ShapeDtypeStruct((t, ROUTE_LANES), F32),
        ],
        compiler_params=_params(("arbitrary",)),
    )(attn_p, attn_s, convn, x, row(attn_g, ATTN_WIDTH), w_out_bf, modr, row(norm2_g, D_MODEL), modr, modr,
      wr, br)


def _dispatch(experts, tm):
    t = experts.shape[0]
    a = t * TOP_K
    nb = -(-(a + N_EXPERTS * (tm - 1)) // tm)
    flat_e = experts.reshape(-1)
    order = jnp.argsort(flat_e).astype(jnp.int32)
    rank = jnp.argsort(order).astype(jnp.int32)
    counts = jnp.sum(flat_e[:, None] == jnp.arange(N_EXPERTS)[None, :], axis=0).astype(jnp.int32)
    starts = jnp.cumsum(counts) - counts
    padded = ((counts + tm - 1) // tm) * tm
    padded_ends = jnp.cumsum(padded)
    padded_starts = padded_ends - padded
    shift = (padded_starts - starts).astype(jnp.int32)
    pos = (rank + shift[flat_e]).reshape(t, TOP_K)
    block_expert = jnp.minimum(
        jnp.sum(padded_ends[None, :] <= (jnp.arange(nb) * tm)[:, None], axis=1), N_EXPERTS - 1).astype(jnp.int32)
    row = jnp.arange(nb * tm, dtype=jnp.int32)
    row_e = jnp.repeat(block_expert, tm)
    sorted_pos = row - shift[row_e]
    valid = sorted_pos < (starts + counts)[row_e]
    src = order[jnp.clip(sorted_pos, 0, a - 1)]
    row_tok = jnp.where(valid, src // TOP_K, row % t).astype(jnp.int32)
    n_valid = (padded_ends[-1] // tm).astype(jnp.int32).reshape(1)
    nonempty = counts > 0
    slot = ((jnp.cumsum(nonempty) - nonempty) % 2).astype(jnp.int32)
    ids = jnp.where(nonempty, jnp.arange(N_EXPERTS), N_EXPERTS)
    nxt = lax.cummin(jnp.concatenate([ids[1:], jnp.array([N_EXPERTS])]), reverse=True)
    nxt = jnp.where(nxt >= N_EXPERTS, -1, nxt).astype(jnp.int32)
    run_info = jnp.concatenate([slot, nxt, (padded_starts // tm).astype(jnp.int32)])
    return row_tok, pos, block_expert, n_valid, run_info


def _unpack_rows(words):
    lo = pltpu.bitcast(words << 16, F32).astype(BF16)
    hi = pltpu.bitcast(words & jnp.uint32(0xFFFF0000), F32).astype(BF16)
    return lo, hi


def _expert_weights(be_ref, ri_ref, hbm_refs, bufs, sem, layer, tn):
    j, i = pl.program_id(0), pl.program_id(1)
    e = be_ref[i]
    slot, nxt, first_blk = ri_ref[e], ri_ref[N_EXPERTS + e], ri_ref[2 * N_EXPERTS + e]

    def copies(expert, s):
        return [pltpu.make_async_copy(h.at[layer, expert, :, pl.ds(pl.multiple_of(j * tn, tn), tn)], b.at[s],
                                      sem.at[n, s])
                for n, (h, b) in enumerate(zip(hbm_refs, bufs))]

    @pl.when(i == first_blk)
    def _():
        @pl.when(i == 0)
        def _():
            for c in copies(e, slot):
                c.start()

        for c in copies(e, slot):
            c.wait()

        @pl.when(nxt >= 0)
        def _():
            for c in copies(nxt, 1 - slot):
                c.start()

    return slot


def _gate_up_body(be_ref, nv_ref, ri_ref, xs_ref, wg_hbm, wu_hbm, a_ref, wg_buf, wu_buf, sem, *, layer, tn):
    i = pl.program_id(1)
    half = D_MODEL // 2

    @pl.when(i < nv_ref[0])
    def _():
        slot = _expert_weights(be_ref, ri_ref, (wg_hbm, wu_hbm), (wg_buf, wu_buf), sem, layer, tn)
        lo, hi = _unpack_rows(xs_ref[...])
        g = (jnp.dot(lo, wg_buf[slot, :half, :].astype(BF16), preferred_element_type=F32)
             + jnp.dot(hi, wg_buf[slot, half:, :].astype(BF16), preferred_element_type=F32))
        u = (jnp.dot(lo, wu_buf[slot, :half, :].astype(BF16), preferred_element_type=F32)
             + jnp.dot(hi, wu_buf[slot, half:, :].astype(BF16), preferred_element_type=F32))
        a_ref[...] = (g * jax.nn.sigmoid(g) * u).astype(a_ref.dtype)

    @pl.when(i >= nv_ref[0])
    def _():
        a_ref[...] = jnp.zeros_like(a_ref)


def _gate_up(xs, e_gate, e_up, block_expert, n_valid, run_info, layer, tm):
    p = xs.shape[0]
    tn = D_EXPERT
    return pl.pallas_call(
        functools.partial(_gate_up_body, layer=layer, tn=tn),
        grid_spec=pltpu.PrefetchScalarGridSpec(
            num_scalar_prefetch=3,
            grid=(D_EXPERT // tn, p // tm),
            in_specs=[
                pl.BlockSpec((tm, D_MODEL // 2), lambda j, i, be, nv, ri: (jnp.minimum(i, nv[0] - 1), 0)),
                pl.BlockSpec(memory_space=pl.ANY),
                pl.BlockSpec(memory_space=pl.ANY),
            ],
            out_specs=pl.BlockSpec((tm, tn), lambda j, i, be, nv, ri: (i, j)),
            scratch_shapes=[pltpu.VMEM((2, D_MODEL, tn), F32), pltpu.VMEM((2, D_MODEL, tn), F32),
                            pltpu.SemaphoreType.DMA((2, 2))],
        ),
        out_shape=jax.ShapeDtypeStruct((p, D_EXPERT), BF16),
        compiler_params=_params(("arbitrary", "arbitrary")),
    )(block_expert, n_valid, run_info, xs, e_gate, e_up)


def _down_body(be_ref, nv_ref, ri_ref, a_ref, wd_hbm, y_ref, wd_buf, sem, *, layer, tn):
    i = pl.program_id(1)

    @pl.when(i < nv_ref[0])
    def _():
        slot = _expert_weights(be_ref, ri_ref, (wd_hbm,), (wd_buf,), sem, layer, tn)
        y_ref[...] = jnp.dot(a_ref[...], wd_buf[slot].astype(BF16), preferred_element_type=F32)

    @pl.when(i >= nv_ref[0])
    def _():
        y_ref[...] = jnp.zeros_like(y_ref)


def _down(a, e_down, block_expert, n_valid, run_info, layer, tm):
    p = a.shape[0]
    tn = D_MODEL
    return pl.pallas_call(
        functools.partial(_down_body, layer=layer, tn=tn),
        grid_spec=pltpu.PrefetchScalarGridSpec(
            num_scalar_prefetch=3,
            grid=(D_MODEL // tn, p // tm),
            in_specs=[
                pl.BlockSpec((tm, D_EXPERT), lambda j, i, be, nv, ri: (jnp.minimum(i, nv[0] - 1), 0)),
                pl.BlockSpec(memory_space=pl.ANY),
            ],
            out_specs=pl.BlockSpec((tm, tn), lambda j, i, be, nv, ri: (i, j)),
            scratch_shapes=[pltpu.VMEM((2, D_EXPERT, tn), F32), pltpu.SemaphoreType.DMA((1, 2))],
        ),
        out_shape=jax.ShapeDtypeStruct((p, D_MODEL), F32),
        compiler_params=_params(("arbitrary", "arbitrary")),
    )(block_expert, n_valid, run_info, a, e_down)


def _moe_residual(x_ref, y0_ref, y1_ref, rf_ref, gate_ref):
    rf = rf_ref[...]
    return x_ref[...] + gate_ref[0] * (rf[:, 0:1] * y0_ref[...] + rf[:, 1:2] * y1_ref[...])


def _combine_body(x_ref, y0_ref, y1_ref, rf_ref, gate_ref, shift_ref, scale_ref, g_ref, o_ref, h_ref):
    x = _moe_residual(x_ref, y0_ref, y1_ref, rf_ref, gate_ref)
    o_ref[...] = x
    h_ref[...] = _norm_mod(x, g_ref, scale_ref, shift_ref)


def _combine(x, y0, y1, rf, modr, modr_next, norm_g_next, lay, tm):
    t = x.shape[0]
    blk = pl.BlockSpec((tm, D_MODEL), lambda i: (i, 0))

    def mod_row(k):
        return pl.BlockSpec((1, 1, D_MODEL), lambda i: (_seq_of_row(i * tm, lay) * 6 + k, 0, 0))

    return pl.pallas_call(
        _combine_body,
        grid=(t // tm,),
        in_specs=[blk, blk, blk, pl.BlockSpec((tm, ROUTE_LANES), lambda i: (i, 0)),
                  mod_row(5), mod_row(0), mod_row(1), pl.BlockSpec((1, D_MODEL), lambda i: (0, 0))],
        out_specs=[blk, blk],
        out_shape=[jax.ShapeDtypeStruct((t, D_MODEL), F32), jax.ShapeDtypeStruct((t, D_MODEL), BF16)],
        compiler_params=_params(("arbitrary",)),
    )(x, y0, y1, rf, modr, modr_next, modr_next, norm_g_next.reshape(1, D_MODEL))


def _final_body(x_ref, y0_ref, y1_ref, rf_ref, gate_ref, g_ref, op_ref, os_ref, *, n_prompt_blocks):
    i = pl.program_id(0)
    x = _moe_residual(x_ref, y0_ref, y1_ref, rf_ref, gate_ref)
    ms = jnp.mean(x * x, axis=-1, keepdims=True)
    y = x * lax.rsqrt(ms + NORM_EPS) * g_ref[...]

    @pl.when(i < n_prompt_blocks)
    def _():
        op_ref[...] = y

    @pl.when(i >= n_prompt_blocks)
    def _():
        os_ref[...] = y


def _final(x, y0, y1, rf, modr, final_g, lay, tm):
    batch, seq, dec_batch, dec_seq = lay
    t = x.shape[0]
    n_prompt = batch * seq
    npb = n_prompt // tm
    blk = pl.BlockSpec((tm, D_MODEL), lambda i: (i, 0))
    return pl.pallas_call(
        functools.partial(_final_body, n_prompt_blocks=npb),
        grid=(t // tm,),
        in_specs=[blk, blk, blk, pl.BlockSpec((tm, ROUTE_LANES), lambda i: (i, 0)),
                  pl.BlockSpec((1, 1, D_MODEL), lambda i: (_seq_of_row(i * tm, lay) * 6 + 5, 0, 0)),
                  pl.BlockSpec((1, D_MODEL), lambda i: (0, 0))],
        out_specs=[
            pl.BlockSpec((tm, D_MODEL), lambda i: (jnp.minimum(i, npb - 1), 0)),
            pl.BlockSpec((tm, D_MODEL), lambda i: (jnp.maximum(i - npb, 0), 0)),
        ],
        out_shape=[
            jax.ShapeDtypeStruct((n_prompt, D_MODEL), F32),
            jax.ShapeDtypeStruct((t - n_prompt, D_MODEL), F32),
        ],
        compiler_params=_params(("arbitrary",)),
    )(x, y0, y1, rf, modr, final_g.reshape(1, D_MODEL))


def _trunk(x_prompt, x_sample, c_all, lay, ada_w, ada_b, norm1_g, w_in, conv_w, conv_b, conv_ln_g, conv_ln_b,
           attn_out_g, conv_out_g, w_out, norm2_g, rg_w, rg_b, re_w, re_b, e_gate, e_up, e_down,
           final_g, *, tm_in=1024, tm_out=512, ts_conv=256, tm_moe=512, tm_res=512):
    batch, seq, dec_batch, dec_seq = lay
    ns = c_all.shape[0]
    ns8 = -(-ns // 8) * 8
    c_pad = jnp.zeros((ns8, D_MODEL), F32).at[:ns].set(c_all)
    mod = _ada_mod(c_pad, ada_w, ada_b)
    modrs = [mod[l, :ns].reshape(ns * 6, 1, D_MODEL) for l in range(DEPTH)]
    bias = _attn_bias()
    n_route = N_GROUPS + N_EXPERTS
    y_prompt = y_sample = None
    x, h = _prologue(x_prompt, x_sample, modrs[0], norm1_g[0], lay, tm_res)
    for l in range(DEPTH):
        modr = modrs[l]
        qkv, u = _proj_in(h, w_in[l].astype(BF16), tm_in)
        attn_p = _attention(qkv, bias, lay, group=0)
        attn_s = _attention(qkv, bias, lay, group=1)
        convn = _conv(u, conv_w[l], conv_b[l], conv_ln_g[l], conv_ln_b[l], conv_out_g[l], lay, ts_conv)
        wr = jnp.zeros((D_MODEL, ROUTE_LANES), F32).at[:, :n_route].set(
            jnp.concatenate([rg_w[l], re_w[l]], axis=1)).astype(BF16)
        br = jnp.zeros((1, ROUTE_LANES), F32).at[0, :n_route].set(jnp.concatenate([rg_b[l], re_b[l]]))
        x_mid, h2, ri, rf = _proj_out(attn_p, attn_s, convn, x, modr, attn_out_g[l], w_out[l].astype(BF16),
                                      norm2_g[l], wr, br, lay, tm_out)
        row_tok, pos, block_expert, n_valid, run_info = _dispatch(ri[:, :TOP_K], tm_moe)
        xs = h2.at[row_tok].get(mode='promise_in_bounds')
        act = _gate_up(xs, e_gate, e_up, block_expert, n_valid, run_info, l, tm_moe)
        ys = _down(act, e_down, block_expert, n_valid, run_info, l, tm_moe)
        y0 = ys.at[pos[:, 0]].get(mode='promise_in_bounds')
        y1 = ys.at[pos[:, 1]].get(mode='promise_in_bounds')
        if l < DEPTH - 1:
            x, h = _combine(x_mid, y0, y1, rf, modr, modrs[l + 1], norm1_g[l + 1], lay, tm_res)
        else:
            y_prompt, y_sample = _final(x_mid, y0, y1, rf, modr, final_g, lay, tm_res)
    return (y_prompt.reshape(batch, seq, D_MODEL), y_sample.reshape(dec_batch, dec_seq, D_MODEL))


def kernel(x_prompt, x_sample, c_prompt, c_sample, ada_w, ada_b, norm1_g, w_in, conv_w, conv_b, conv_ln_g, conv_ln_b, attn_out_g, conv_out_g, w_out, norm2_g, rg_w, rg_b, re_w, re_b, e_gate, e_up, e_down, final_g):
    batch, seq, _ = x_prompt.shape
    dec_batch, dec_seq, _ = x_sample.shape
    lay = (batch, seq, dec_batch, dec_seq)
    c_all = jnp.concatenate([c_prompt, c_sample], axis=0)
    return _trunk(x_prompt.reshape(batch * seq, D_MODEL), x_sample.reshape(dec_batch * dec_seq, D_MODEL),
                  c_all, lay, ada_w, ada_b, norm1_g, w_in, conv_w, conv_b, conv_ln_g, conv_ln_b,
                  attn_out_g, conv_out_g, w_out, norm2_g, rg_w, rg_b, re_w, re_b,
                  e_gate, e_up, e_down, final_g)
```

```python
import functools
import math

import jax
import jax.numpy as jnp
from jax import lax
from jax.experimental import pallas as pl
from jax.experimental.pallas import tpu as pltpu

D_MODEL = 2048
DEPTH = 4
N_HEADS = 8
HEAD_DIM = 128
ATTN_WIDTH = N_HEADS * HEAD_DIM
CONV_CH = D_MODEL - ATTN_WIDTH
CONV_WIDTH = 31
DILATED_CONFIGS = ((128, 1), (512, 4), (2048, 16))
N_GROUPS = 4
EXPERTS_PER_GROUP = 8
N_EXPERTS = N_GROUPS * EXPERTS_PER_GROUP
TOP_K = 2
D_EXPERT = 1024
NORM_EPS = 1e-6
MASK_VALUE = -1e30

LANES = 128
ATT_TQ = 128
ATT_TK = 256
ATT_HALF = 64
ATT_STAGE = 256
ATT_UNROLL = 8
CONV_HALO = 16
ROUTE_LANES = 128
PROJ_OUT_SUB = 256
VMEM_LIMIT = 56 * 1024 * 1024

F32 = jnp.float32
BF16 = jnp.bfloat16


def _params(sem, vmem=VMEM_LIMIT):
    return pltpu.CompilerParams(dimension_semantics=sem, vmem_limit_bytes=vmem)


def _seq_of_row(row, lay):
    batch, seq, _, dec_seq = lay
    n_prompt = batch * seq
    return jnp.where(row < n_prompt, row // seq, batch + (row - n_prompt) // dec_seq)


def _ada_body(c_ref, w_ref, b_ref, o_ref):
    c = c_ref[...]
    sc = c * jax.nn.sigmoid(c)
    o_ref[0] = jnp.dot(sc, w_ref[0], preferred_element_type=F32) + b_ref[0]


def _ada_mod(c_pad, ada_w, ada_b):
    ns8 = c_pad.shape[0]
    tn = 1024
    nj = 6 * D_MODEL // tn
    return pl.pallas_call(
        _ada_body,
        grid=(DEPTH, nj),
        in_specs=[
            pl.BlockSpec((ns8, D_MODEL), lambda l, j: (0, 0)),
            pl.BlockSpec((1, D_MODEL, tn), lambda l, j: (l, 0, j)),
            pl.BlockSpec((1, 1, tn), lambda l, j: (l, 0, j)),
        ],
        out_specs=pl.BlockSpec((1, ns8, tn), lambda l, j: (l, 0, j)),
        out_shape=jax.ShapeDtypeStruct((DEPTH, ns8, 6 * D_MODEL), F32),
        compiler_params=_params(("arbitrary", "arbitrary")),
    )(c_pad, ada_w, ada_b.reshape(DEPTH, 1, 6 * D_MODEL))


def _pack_rows(v):
    bits = pltpu.bitcast(v.astype(BF16).astype(F32), jnp.uint32)
    half = v.shape[1] // 2
    return (bits[:, :half] >> 16) | (bits[:, half:] & jnp.uint32(0xFFFF0000))


def _unpack_rows(words):
    return (pltpu.bitcast(words << 16, F32), pltpu.bitcast(words & jnp.uint32(0xFFFF0000), F32))


def _norm_mod(x, g_ref, scale_ref, shift_ref):
    ms = jnp.mean(x * x, axis=-1, keepdims=True)
    y = x * lax.rsqrt(ms + NORM_EPS) * g_ref[...]
    return (y * (1.0 + scale_ref[0]) + shift_ref[0]).astype(BF16)


def _prologue_body(xp_ref, xs_ref, shift_ref, scale_ref, g_ref, x_ref, h_ref, *, n_prompt_blocks):
    x = jnp.where(pl.program_id(0) < n_prompt_blocks, xp_ref[...], xs_ref[...])
    x_ref[...] = x
    h_ref[...] = _norm_mod(x, g_ref, scale_ref, shift_ref)


def _prologue(x_prompt, x_sample, modr, norm_g, lay, tm):
    n_prompt, n_sample = x_prompt.shape[0], x_sample.shape[0]
    t = n_prompt + n_sample
    npb = n_prompt // tm
    blk = pl.BlockSpec((tm, D_MODEL), lambda i: (i, 0))
    return pl.pallas_call(
        functools.partial(_prologue_body, n_prompt_blocks=npb),
        grid=(t // tm,),
        in_specs=[
            pl.BlockSpec((tm, D_MODEL), lambda i: (jnp.minimum(i, npb - 1), 0)),
            pl.BlockSpec((tm, D_MODEL), lambda i: (jnp.maximum(i - npb, 0), 0)),
            pl.BlockSpec((1, 1, D_MODEL), lambda i: (_seq_of_row(i * tm, lay) * 6 + 0, 0, 0)),
            pl.BlockSpec((1, 1, D_MODEL), lambda i: (_seq_of_row(i * tm, lay) * 6 + 1, 0, 0)),
            pl.BlockSpec((1, D_MODEL), lambda i: (0, 0)),
        ],
        out_specs=[blk, blk],
        out_shape=[jax.ShapeDtypeStruct((t, D_MODEL), F32), jax.ShapeDtypeStruct((t, D_MODEL), BF16)],
        compiler_params=_params(("arbitrary",)),
    )(x_prompt, x_sample, modr, modr, norm_g.reshape(1, D_MODEL))


def _proj_in_body(h_ref, wqkv_ref, wga_ref, wgb_ref, qkv_ref, u_ref):
    j = pl.program_id(1)

    @pl.when(j < 3)
    def _():
        r = jnp.dot(h_ref[...], wqkv_ref[...], preferred_element_type=F32)
        r = r * jnp.where(j == 0, HEAD_DIM ** -0.5, 1.0).astype(F32)
        for hh in range(N_HEADS):
            qkv_ref[hh] = r[:, hh * HEAD_DIM:(hh + 1) * HEAD_DIM]

    @pl.when(j == 3)
    def _():
        h = h_ref[...]
        a = jnp.dot(h, wga_ref[...], preferred_element_type=F32)
        b = jnp.dot(h, wgb_ref[...], preferred_element_type=F32)
        u_ref[...] = a * jax.nn.sigmoid(b)


def _proj_in(h, w_in_bf, tm):
    t = h.shape[0]
    tn = ATTN_WIDTH
    assert CONV_CH == tn
    resident = dict(pipeline_mode=pl.Buffered(1))
    return pl.pallas_call(
        _proj_in_body,
        grid=(t // tm, 4),
        in_specs=[
            pl.BlockSpec((tm, D_MODEL), lambda i, j: (i, 0)),
            pl.BlockSpec((D_MODEL, tn), lambda i, j: (0, jnp.minimum(j, 2))),
            pl.BlockSpec((D_MODEL, tn), lambda i, j: (0, 3), **resident),
            pl.BlockSpec((D_MODEL, tn), lambda i, j: (0, 4), **resident),
        ],
        out_specs=[
            pl.BlockSpec((N_HEADS, tm, HEAD_DIM), lambda i, j: (jnp.minimum(j, 2), i, 0)),
            pl.BlockSpec((tm, CONV_CH), lambda i, j: (i, 0)),
        ],
        out_shape=[
            jax.ShapeDtypeStruct((3 * N_HEADS, t, HEAD_DIM), F32),
            jax.ShapeDtypeStruct((t, CONV_CH), F32),
        ],
        compiler_params=_params(("arbitrary", "arbitrary")),
    )(h, w_in_bf, w_in_bf, w_in_bf)


def _alibi_slopes():
    return jnp.exp2(-8.0 * jnp.arange(1, N_HEADS + 1, dtype=F32) / N_HEADS)


def _attn_bias():
    i = jnp.arange(ATT_TQ)[:, None]
    j = jnp.arange(ATT_TK)[None, :]
    out = []
    for _, d in DILATED_CONFIGS:
        for v in range(3):
            rel = jnp.abs(j - ATT_HALF * v - i)
            dist = (d * rel).astype(F32)
            b = -_alibi_slopes()[:, None, None] * dist[None]
            out.append(jnp.where((rel <= ATT_HALF)[None], b, MASK_VALUE))
    return jnp.stack(out, axis=1)


def _strided_rows(start, size, stride):
    return pl.ds(start, size, stride=stride) if stride > 1 else pl.ds(start, size)


def _attn_body(q_ref, k_ref, v_ref, bias_ref, *rest, seq_len):
    o_ref, lse_scr, qs, ks, vs = rest[-5:]
    order = sorted(range(len(DILATED_CONFIGS)), key=lambda c: -DILATED_CONFIGS[c][1])
    for step, c in enumerate(order):
        d = DILATED_CONFIGS[c][1]
        sub_len = seq_len // d
        first, last = step == 0, step == len(order) - 1

        nstage = sub_len // ATT_STAGE
        nt = sub_len // ATT_TQ

        def stage(n, cr, d=d, sub_len=sub_len, nstage=nstage):
            r = lax.div(n, nstage)
            ch = lax.rem(n, nstage)
            dst = pl.ds(pl.multiple_of(n * ATT_STAGE, ATT_STAGE), ATT_STAGE)
            src = _strided_rows(r + ch * (ATT_STAGE * d), ATT_STAGE, d)
            qs[dst, :] = q_ref[0, src, :].astype(BF16)
            ks[dst, :] = k_ref[0, src, :].astype(BF16)
            vs[dst, :] = v_ref[0, src, :].astype(BF16)
            return cr

        lax.fori_loop(0, d * nstage, stage, 0, unroll=2)

        def tile(n, ct, c=c, d=d, sub_len=sub_len, nt=nt, first=first, last=last):
            r = lax.div(n, nt)
            l0 = lax.rem(n, nt) * ATT_TQ
            k0 = jnp.clip(l0 - ATT_HALF, 0, sub_len - ATT_TK)
            var = lax.div(l0 - k0, ATT_HALF)
            base = r * sub_len
            q = qs[pl.ds(pl.multiple_of(base + l0, ATT_TQ), ATT_TQ), :]
            k = ks[pl.ds(pl.multiple_of(base + k0, ATT_HALF), ATT_TK), :]
            v = vs[pl.ds(pl.multiple_of(base + k0, ATT_HALF), ATT_TK), :]
            s = lax.dot_general(q, k, (((1,), (1,)), ((), ())), preferred_element_type=F32)
            s = s + bias_ref[0, 3 * c + var]
            m = jnp.max(s, axis=-1, keepdims=True)
            p = jnp.exp(s - m)
            l = jnp.sum(p, axis=-1, keepdims=True)
            o = jnp.dot(p.astype(BF16), v, preferred_element_type=F32)
            mb = jnp.broadcast_to(m, (ATT_TQ, LANES))
            lb = jnp.broadcast_to(l, (ATT_TQ, LANES))
            rows = _strided_rows(r + l0 * d, ATT_TQ, d)
            if first:
                o_ref[rows, :] = o / lb
                lse_scr[rows, :] = mb + jnp.log(lb)
            else:
                la = lse_scr[rows, :]
                mx = jnp.maximum(la, mb)
                wa = jnp.exp(la - mx)
                wb = jnp.exp(mb - mx)
                den = wa + wb * lb
                o_ref[rows, :] = (wa * o_ref[rows, :] + wb * o) / den
                if not last:
                    lse_scr[rows, :] = mx + jnp.log(den)
            return ct

        lax.fori_loop(0, d * nt, tile, 0, unroll=ATT_UNROLL)


def _attention(qkv, bias, lay, *, group):
    batch, seq, dec_batch, dec_seq = lay
    if group == 0:
        nb, s_len, blk0 = batch, seq, 0
    else:
        assert (batch * seq) % dec_seq == 0
        nb, s_len, blk0 = dec_batch, dec_seq, batch * seq // dec_seq
    assert s_len % (ATT_TK * DILATED_CONFIGS[-1][1]) == 0

    def hm(part):
        return lambda b, h: (part * N_HEADS + h, blk0 + b, 0)

    return pl.pallas_call(
        functools.partial(_attn_body, seq_len=s_len),
        grid=(nb, N_HEADS),
        in_specs=[
            pl.BlockSpec((1, s_len, HEAD_DIM), hm(0)),
            pl.BlockSpec((1, s_len, HEAD_DIM), hm(1)),
            pl.BlockSpec((1, s_len, HEAD_DIM), hm(2)),
            pl.BlockSpec((1, bias.shape[1], ATT_TQ, ATT_TK), lambda b, h: (h, 0, 0, 0)),
        ],
        out_specs=pl.BlockSpec((s_len, HEAD_DIM), lambda b, h: (b, h)),
        out_shape=jax.ShapeDtypeStruct((nb * s_len, ATTN_WIDTH), F32),
        scratch_shapes=[pltpu.VMEM((s_len, LANES), F32)] + [pltpu.VMEM((s_len, HEAD_DIM), BF16)] * 3,
        compiler_params=_params(("arbitrary", "arbitrary")),
    )(qkv, qkv, qkv, bias)


def _conv_body(prev_ref, cur_ref, next_ref, w_ref, cb_ref, lg_ref, lb_ref, og_ref, o_ref,
               pad_scr, cv_scr, *, lay, ts):
    batch, seq, dec_batch, dec_seq = lay
    n_prompt = batch * seq
    row0 = pl.program_id(0) * ts
    row1 = row0 + ts
    pos0 = jnp.where(row0 < n_prompt, row0 % seq, (row0 - n_prompt) % dec_seq)
    pos1 = jnp.where(row1 <= n_prompt, row1 % seq, (row1 - n_prompt) % dec_seq)
    nch = CONV_CH // LANES
    lanes = [slice(cc * LANES, (cc + 1) * LANES) for cc in range(nch)]
    for cc in range(nch):
        pad_scr[cc, 0:CONV_HALO, :] = jnp.where(pos0 != 0, prev_ref[:, lanes[cc]], 0.0)
        pad_scr[cc, CONV_HALO:CONV_HALO + ts, :] = cur_ref[:, lanes[cc]]
        pad_scr[cc, CONV_HALO + ts:, :] = jnp.where(pos1 != 0, next_ref[:, lanes[cc]], 0.0)

    rc = 128
    base = CONV_HALO - CONV_WIDTH // 2

    def lane_chunk(cc, carry):
        for r in range(ts // rc):
            for par in range(2):
                acc = jnp.zeros((rc // 2, LANES), F32)
                for j in range(CONV_WIDTH):
                    acc = acc + (pad_scr[cc, pl.ds(base + r * rc + par + j, rc // 2, stride=2), :]
                                 * w_ref[cc, pl.ds(j, 1), :])
                cv_scr[cc, pl.ds(r * rc + par, rc // 2, stride=2), :] = acc
        return carry

    lax.fori_loop(0, nch, lane_chunk, 0)

    rn = 32
    inv_c = 1.0 / CONV_CH

    def row_chunk(r, carry):
        r0 = pl.multiple_of(r * rn, rn)
        u = [cv_scr[cc, pl.ds(r0, rn), :] + cb_ref[:, lanes[cc]] for cc in range(nch)]
        mu = jnp.sum(sum(u), axis=-1, keepdims=True) * inv_c
        u = [a - mu for a in u]
        var = jnp.sum(sum(a * a for a in u), axis=-1, keepdims=True) * inv_c
        rstd = lax.rsqrt(var + NORM_EPS)
        y = [a * rstd * lg_ref[:, lanes[cc]] + lb_ref[:, lanes[cc]] for cc, a in enumerate(u)]
        y = [a * jax.nn.sigmoid(a) for a in y]
        ms = jnp.sum(sum(a * a for a in y), axis=-1, keepdims=True) * inv_c
        rr = lax.rsqrt(ms + NORM_EPS)
        for cc, a in enumerate(y):
            o_ref[pl.ds(r0, rn), lanes[cc]] = (a * rr * og_ref[:, lanes[cc]]).astype(o_ref.dtype)
        return carry

    lax.fori_loop(0, ts // rn, row_chunk, 0, unroll=4)


def _conv(u, conv_w, conv_b, ln_g, ln_b, out_g, lay, ts):
    t = u.shape[0]
    hb = ts // CONV_HALO
    nhb = t // CONV_HALO
    nch = CONV_CH // LANES
    w_slabs = conv_w.reshape(CONV_WIDTH, nch, LANES).transpose(1, 0, 2)
    row = lambda a: a.reshape(1, CONV_CH)
    return pl.pallas_call(
        functools.partial(_conv_body, lay=lay, ts=ts),
        grid=(t // ts,),
        in_specs=[
            pl.BlockSpec((CONV_HALO, CONV_CH), lambda i: (jnp.maximum(i * hb - 1, 0), 0)),
            pl.BlockSpec((ts, CONV_CH), lambda i: (i, 0)),
            pl.BlockSpec((CONV_HALO, CONV_CH), lambda i: (jnp.minimum((i + 1) * hb, nhb - 1), 0)),
            pl.BlockSpec((nch, CONV_WIDTH, LANES), lambda i: (0, 0, 0)),
            pl.BlockSpec((1, CONV_CH), lambda i: (0, 0)),
            pl.BlockSpec((1, CONV_CH), lambda i: (0, 0)),
            pl.BlockSpec((1, CONV_CH), lambda i: (0, 0)),
            pl.BlockSpec((1, CONV_CH), lambda i: (0, 0)),
        ],
        out_specs=pl.BlockSpec((ts, CONV_CH), lambda i: (i, 0)),
        out_shape=jax.ShapeDtypeStruct((t, CONV_CH), BF16),
        scratch_shapes=[pltpu.VMEM((nch, ts + 2 * CONV_HALO, LANES), F32),
                        pltpu.VMEM((nch, ts, LANES), F32)],
        compiler_params=_params(("arbitrary",)),
    )(u, u, u, w_slabs, row(conv_b), row(ln_g), row(ln_b), row(out_g))


def _proj_out_body(attn_p_ref, attn_s_ref, conv_ref, x_ref, ag_ref, wout_ref, gate_ref, g2_ref, shift_ref,
                   scale_ref, wr_ref, br_ref, xmid_ref, h_ref, ri_ref, rf_ref, *, n_prompt_blocks):
    for s in range(attn_p_ref.shape[0] // PROJ_OUT_SUB):
        _proj_out_rows(slice(s * PROJ_OUT_SUB, (s + 1) * PROJ_OUT_SUB), attn_p_ref, attn_s_ref, conv_ref,
                       x_ref, ag_ref, wout_ref, gate_ref, g2_ref, shift_ref, scale_ref, wr_ref, br_ref,
                       xmid_ref, h_ref, ri_ref, rf_ref, n_prompt_blocks)


def _proj_out_rows(rs, attn_p_ref, attn_s_ref, conv_ref, x_ref, ag_ref, wout_ref, gate_ref, g2_ref, shift_ref,
                   scale_ref, wr_ref, br_ref, xmid_ref, h_ref, ri_ref, rf_ref, n_prompt_blocks):
    a = jnp.where(pl.program_id(0) < n_prompt_blocks, attn_p_ref[rs, :], attn_s_ref[rs, :])
    ms = jnp.mean(a * a, axis=-1, keepdims=True)
    an = (a * lax.rsqrt(ms + NORM_EPS) * ag_ref[...]).astype(BF16)
    o = jnp.dot(an, wout_ref[0:ATTN_WIDTH, :], preferred_element_type=F32)
    o = o + jnp.dot(conv_ref[rs, :], wout_ref[ATTN_WIDTH:, :], preferred_element_type=F32)
    x = x_ref[rs, :] + gate_ref[0] * o
    xmid_ref[rs, :] = x
    hb = _norm_mod(x, g2_ref, scale_ref, shift_ref)
    h_ref[rs, :] = _pack_rows(hb)

    lg = jnp.dot(hb, wr_ref[...], preferred_element_type=F32) + br_ref[...]
    lane = lax.broadcasted_iota(jnp.int32, lg.shape, 1)
    lane_f = lane.astype(F32)
    neg = jnp.float32(-jnp.inf)
    big = jnp.float32(ROUTE_LANES)
    gmask = lane < N_GROUPS
    lgm = jnp.where(gmask, lg, neg)
    gmax = jnp.max(lgm, axis=-1, keepdims=True)
    gidx = jnp.min(jnp.where(lgm == gmax, lane_f, big), axis=-1, keepdims=True).astype(jnp.int32)
    gsum = jnp.sum(jnp.where(gmask, jnp.exp(lg - gmax), 0.0), axis=-1, keepdims=True)
    pg = 1.0 / gsum
    lo = N_GROUPS + gidx * EXPERTS_PER_GROUP
    emask = (lane >= lo) & (lane < lo + EXPERTS_PER_GROUP)
    le1 = jnp.where(emask, lg, neg)
    e1 = jnp.max(le1, axis=-1, keepdims=True)
    i1 = jnp.min(jnp.where(le1 == e1, lane_f, big), axis=-1, keepdims=True)
    le2 = jnp.where(lane_f == i1, neg, le1)
    e2 = jnp.max(le2, axis=-1, keepdims=True)
    i2 = jnp.min(jnp.where(le2 == e2, lane_f, big), axis=-1, keepdims=True)
    r = jnp.exp(e2 - e1)
    w1 = pg / (1.0 + r)
    w2 = pg * r / (1.0 + r)
    id1 = (i1 - N_GROUPS).astype(jnp.int32)
    id2 = (i2 - N_GROUPS).astype(jnp.int32)
    ri_ref[rs, :] = jnp.where(lane == 0, id1, jnp.where(lane == 1, id2, 0))
    rf_ref[rs, :] = jnp.where(lane == 0, w1, jnp.where(lane == 1, w2, 0.0))


def _proj_out(attn_p, attn_s, convn, x, modr, attn_g, w_out_bf, norm2_g, wr, br, lay, tm):
    t = x.shape[0]
    npb = attn_p.shape[0] // tm

    def seq(i):
        return _seq_of_row(i * tm, lay)

    row = lambda a, n: a.reshape(1, n)
    return pl.pallas_call(
        functools.partial(_proj_out_body, n_prompt_blocks=npb),
        grid=(t // tm,),
        in_specs=[
            pl.BlockSpec((tm, ATTN_WIDTH), lambda i: (jnp.minimum(i, npb - 1), 0)),
            pl.BlockSpec((tm, ATTN_WIDTH), lambda i: (jnp.maximum(i - npb, 0), 0)),
            pl.BlockSpec((tm, CONV_CH), lambda i: (i, 0)),
            pl.BlockSpec((tm, D_MODEL), lambda i: (i, 0)),
            pl.BlockSpec((1, ATTN_WIDTH), lambda i: (0, 0)),
            pl.BlockSpec((D_MODEL, D_MODEL), lambda i: (0, 0), pipeline_mode=pl.Buffered(1)),
            pl.BlockSpec((1, 1, D_MODEL), lambda i: (seq(i) * 6 + 2, 0, 0)),
            pl.BlockSpec((1, D_MODEL), lambda i: (0, 0)),
            pl.BlockSpec((1, 1, D_MODEL), lambda i: (seq(i) * 6 + 3, 0, 0)),
            pl.BlockSpec((1, 1, D_MODEL), lambda i: (seq(i) * 6 + 4, 0, 0)),
            pl.BlockSpec((D_MODEL, ROUTE_LANES), lambda i: (0, 0)),
            pl.BlockSpec((1, ROUTE_LANES), lambda i: (0, 0)),
        ],
        out_specs=[
            pl.BlockSpec((tm, D_MODEL), lambda i: (i, 0)),
            pl.BlockSpec((tm, D_MODEL // 2), lambda i: (i, 0)),
            pl.BlockSpec((tm, ROUTE_LANES), lambda i: (i, 0)),
            pl.BlockSpec((tm, ROUTE_LANES), lambda i: (i, 0)),
        ],
        out_shape=[
            jax.ShapeDtypeStruct((t, D_MODEL), F32),
            jax.ShapeDtypeStruct((t, D_MODEL // 2), jnp.uint32),
            jax.ShapeDtypeStruct((t, ROUTE_LANES), jnp.int32),
            jax.ShapeDtypeStruct((t, ROUTE_LANES), F32),
        ],
        compiler_params=_params(("arbitrary",)),
    )(attn_p, attn_s, convn, x, row(attn_g, ATTN_WIDTH), w_out_bf, modr, row(norm2_g, D_MODEL), modr, modr,
      wr, br)


def _dispatch(experts, tm):
    t = experts.shape[0]
    a = t * TOP_K
    nb = -(-(a + N_EXPERTS * (tm - 1)) // tm)
    flat_e = experts.reshape(-1)
    order = jnp.argsort(flat_e).astype(jnp.int32)
    rank = jnp.argsort(order).astype(jnp.int32)
    counts = jnp.sum(flat_e[:, None] == jnp.arange(N_EXPERTS)[None, :], axis=0).astype(jnp.int32)
    starts = jnp.cumsum(counts) - counts
    padded = ((counts + tm - 1) // tm) * tm
    padded_ends = jnp.cumsum(padded)
    padded_starts = padded_ends - padded
    shift = (padded_starts - starts).astype(jnp.int32)
    pos = (rank + shift[flat_e]).reshape(t, TOP_K)
    block_expert = jnp.minimum(
        jnp.sum(padded_ends[None, :] <= (jnp.arange(nb) * tm)[:, None], axis=1), N_EXPERTS - 1).astype(jnp.int32)
    row = jnp.arange(nb * tm, dtype=jnp.int32)
    row_e = jnp.repeat(block_expert, tm)
    sorted_pos = row - shift[row_e]
    valid = sorted_pos < (starts + counts)[row_e]
    src = order[jnp.clip(sorted_pos, 0, a - 1)]
    row_tok = jnp.where(valid, src // TOP_K, row % t).astype(jnp.int32)
    n_valid = (padded_ends[-1] // tm).astype(jnp.int32).reshape(1)
    nonempty = counts > 0
    slot = ((jnp.cumsum(nonempty) - nonempty) % 2).astype(jnp.int32)
    ids = jnp.where(nonempty, jnp.arange(N_EXPERTS), N_EXPERTS)
    nxt = lax.cummin(jnp.concatenate([ids[1:], jnp.array([N_EXPERTS])]), reverse=True)
    nxt = jnp.where(nxt >= N_EXPERTS, -1, nxt).astype(jnp.int32)
    run_info = jnp.concatenate([slot, nxt, (padded_starts // tm).astype(jnp.int32)])
    return row_tok, pos, block_expert, n_valid, run_info


def _expert_weights(be_ref, ri_ref, hbm_refs, bufs, sem, layer, tn):
    j, i = pl.program_id(0), pl.program_id(1)
    e = be_ref[i]
    slot, nxt, first_blk = ri_ref[e], ri_ref[N_EXPERTS + e], ri_ref[2 * N_EXPERTS + e]

    def copies(expert, s):
        return [pltpu.make_async_copy(h.at[layer, expert, :, pl.ds(pl.multiple_of(j * tn, tn), tn)], b.at[s],
                                      sem.at[n, s])
                for n, (h, b) in enumerate(zip(hbm_refs, bufs))]

    @pl.when(i == first_blk)
    def _():
        @pl.when(i == 0)
        def _():
            for c in copies(e, slot):
                c.start()

        for c in copies(e, slot):
            c.wait()

        @pl.when(nxt >= 0)
        def _():
            for c in copies(nxt, 1 - slot):
                c.start()

    return slot


def _gate_up_body(be_ref, nv_ref, ri_ref, xs_ref, wg_hbm, wu_hbm, a_ref, wg_buf, wu_buf, sem, *, layer, tn):
    i = pl.program_id(1)
    half = D_MODEL // 2

    @pl.when(i < nv_ref[0])
    def _():
        slot = _expert_weights(be_ref, ri_ref, (wg_hbm, wu_hbm), (wg_buf, wu_buf), sem, layer, tn)
        lo, hi = (v.astype(BF16) for v in _unpack_rows(xs_ref[...]))
        g = (jnp.dot(lo, wg_buf[slot, :half, :].astype(BF16), preferred_element_type=F32)
             + jnp.dot(hi, wg_buf[slot, half:, :].astype(BF16), preferred_element_type=F32))
        u = (jnp.dot(lo, wu_buf[slot, :half, :].astype(BF16), preferred_element_type=F32)
             + jnp.dot(hi, wu_buf[slot, half:, :].astype(BF16), preferred_element_type=F32))
        a_ref[...] = (g * jax.nn.sigmoid(g) * u).astype(a_ref.dtype)

    @pl.when(i >= nv_ref[0])
    def _():
        a_ref[...] = jnp.zeros_like(a_ref)


def _gate_up(xs, e_gate, e_up, block_expert, n_valid, run_info, layer, tm):
    p = xs.shape[0]
    tn = D_EXPERT
    return pl.pallas_call(
        functools.partial(_gate_up_body, layer=layer, tn=tn),
        grid_spec=pltpu.PrefetchScalarGridSpec(
            num_scalar_prefetch=3,
            grid=(D_EXPERT // tn, p // tm),
            in_specs=[
                pl.BlockSpec((tm, D_MODEL // 2), lambda j, i, be, nv, ri: (jnp.minimum(i, nv[0] - 1), 0)),
                pl.BlockSpec(memory_space=pl.ANY),
                pl.BlockSpec(memory_space=pl.ANY),
            ],
            out_specs=pl.BlockSpec((tm, tn), lambda j, i, be, nv, ri: (i, j)),
            scratch_shapes=[pltpu.VMEM((2, D_MODEL, tn), F32), pltpu.VMEM((2, D_MODEL, tn), F32),
                            pltpu.SemaphoreType.DMA((2, 2))],
        ),
        out_shape=jax.ShapeDtypeStruct((p, D_EXPERT), BF16),
        compiler_params=_params(("arbitrary", "arbitrary")),
    )(block_expert, n_valid, run_info, xs, e_gate, e_up)


def _down_body(be_ref, nv_ref, ri_ref, a_ref, wd_hbm, y_ref, wd_buf, sem, *, layer, tn):
    i = pl.program_id(1)

    @pl.when(i < nv_ref[0])
    def _():
        slot = _expert_weights(be_ref, ri_ref, (wd_hbm,), (wd_buf,), sem, layer, tn)
        y_ref[...] = _pack_rows(jnp.dot(a_ref[...], wd_buf[slot].astype(BF16), preferred_element_type=F32))

    @pl.when(i >= nv_ref[0])
    def _():
        y_ref[...] = jnp.zeros_like(y_ref)


def _down(a, e_down, block_expert, n_valid, run_info, layer, tm):
    p = a.shape[0]
    tn = D_MODEL
    return pl.pallas_call(
        functools.partial(_down_body, layer=layer, tn=tn),
        grid_spec=pltpu.PrefetchScalarGridSpec(
            num_scalar_prefetch=3,
            grid=(D_MODEL // tn, p // tm),
            in_specs=[
                pl.BlockSpec((tm, D_EXPERT), lambda j, i, be, nv, ri: (jnp.minimum(i, nv[0] - 1), 0)),
                pl.BlockSpec(memory_space=pl.ANY),
            ],
            out_specs=pl.BlockSpec((tm, tn // 2), lambda j, i, be, nv, ri: (i, j)),
            scratch_shapes=[pltpu.VMEM((2, D_EXPERT, tn), F32), pltpu.SemaphoreType.DMA((1, 2))],
        ),
        out_shape=jax.ShapeDtypeStruct((p, D_MODEL // 2), jnp.uint32),
        compiler_params=_params(("arbitrary", "arbitrary")),
    )(block_expert, n_valid, run_info, a, e_down)


def _moe_residual(x_ref, y0_ref, y1_ref, rf_ref, gate_ref):
    rf = rf_ref[...]
    y0 = jnp.concatenate(_unpack_rows(y0_ref[...]), axis=1)
    y1 = jnp.concatenate(_unpack_rows(y1_ref[...]), axis=1)
    return x_ref[...] + gate_ref[0] * (rf[:, 0:1] * y0 + rf[:, 1:2] * y1)


def _combine_body(x_ref, y0_ref, y1_ref, rf_ref, gate_ref, shift_ref, scale_ref, g_ref, o_ref, h_ref):
    x = _moe_residual(x_ref, y0_ref, y1_ref, rf_ref, gate_ref)
    o_ref[...] = x
    h_ref[...] = _norm_mod(x, g_ref, scale_ref, shift_ref)


def _combine(x, y0, y1, rf, modr, modr_next, norm_g_next, lay, tm):
    t = x.shape[0]
    blk = pl.BlockSpec((tm, D_MODEL), lambda i: (i, 0))
    yblk = pl.BlockSpec((tm, D_MODEL // 2), lambda i: (i, 0))

    def mod_row(k):
        return pl.BlockSpec((1, 1, D_MODEL), lambda i: (_seq_of_row(i * tm, lay) * 6 + k, 0, 0))

    return pl.pallas_call(
        _combine_body,
        grid=(t // tm,),
        in_specs=[blk, yblk, yblk, pl.BlockSpec((tm, ROUTE_LANES), lambda i: (i, 0)),
                  mod_row(5), mod_row(0), mod_row(1), pl.BlockSpec((1, D_MODEL), lambda i: (0, 0))],
        out_specs=[blk, blk],
        out_shape=[jax.ShapeDtypeStruct((t, D_MODEL), F32), jax.ShapeDtypeStruct((t, D_MODEL), BF16)],
        compiler_params=_params(("arbitrary",)),
    )(x, y0, y1, rf, modr, modr_next, modr_next, norm_g_next.reshape(1, D_MODEL))


def _final_body(x_ref, y0_ref, y1_ref, rf_ref, gate_ref, g_ref, op_ref, os_ref, *, n_prompt_blocks):
    i = pl.program_id(0)
    x = _moe_residual(x_ref, y0_ref, y1_ref, rf_ref, gate_ref)
    ms = jnp.mean(x * x, axis=-1, keepdims=True)
    y = x * lax.rsqrt(ms + NORM_EPS) * g_ref[...]

    @pl.when(i < n_prompt_blocks)
    def _():
        op_ref[...] = y

    @pl.when(i >= n_prompt_blocks)
    def _():
        os_ref[...] = y


def _final(x, y0, y1, rf, modr, final_g, lay, tm):
    batch, seq, dec_batch, dec_seq = lay
    t = x.shape[0]
    n_prompt = batch * seq
    npb = n_prompt // tm
    blk = pl.BlockSpec((tm, D_MODEL), lambda i: (i, 0))
    yblk = pl.BlockSpec((tm, D_MODEL // 2), lambda i: (i, 0))
    return pl.pallas_call(
        functools.partial(_final_body, n_prompt_blocks=npb),
        grid=(t // tm,),
        in_specs=[blk, yblk, yblk, pl.BlockSpec((tm, ROUTE_LANES), lambda i: (i, 0)),
                  pl.BlockSpec((1, 1, D_MODEL), lambda i: (_seq_of_row(i * tm, lay) * 6 + 5, 0, 0)),
                  pl.BlockSpec((1, D_MODEL), lambda i: (0, 0))],
        out_specs=[
            pl.BlockSpec((tm, D_MODEL), lambda i: (jnp.minimum(i, npb - 1), 0)),
            pl.BlockSpec((tm, D_MODEL), lambda i: (jnp.maximum(i - npb, 0), 0)),
        ],
        out_shape=[
            jax.ShapeDtypeStruct((n_prompt, D_MODEL), F32),
            jax.ShapeDtypeStruct((t - n_prompt, D_MODEL), F32),
        ],
        compiler_params=_params(("arbitrary",)),
    )(x, y0, y1, rf, modr, final_g.reshape(1, D_MODEL))


def _trunk(x_prompt, x_sample, c_all, lay, ada_w, ada_b, norm1_g, w_in, conv_w, conv_b, conv_ln_g, conv_ln_b,
           attn_out_g, conv_out_g, w_out, norm2_g, rg_w, rg_b, re_w, re_b, e_gate, e_up, e_down,
           final_g, *, tm_in=1024, tm_out=512, ts_conv=256, tm_moe=512, tm_res=512):
    batch, seq, dec_batch, dec_seq = lay
    ns = c_all.shape[0]
    ns8 = -(-ns // 8) * 8
    c_pad = jnp.zeros((ns8, D_MODEL), F32).at[:ns].set(c_all)
    mod = _ada_mod(c_pad, ada_w, ada_b)
    modrs = [mod[l, :ns].reshape(ns * 6, 1, D_MODEL) for l in range(DEPTH)]
    bias = _attn_bias()
    n_route = N_GROUPS + N_EXPERTS
    y_prompt = y_sample = None
    x, h = _prologue(x_prompt, x_sample, modrs[0], norm1_g[0], lay, tm_res)
    for l in range(DEPTH):
        modr = modrs[l]
        qkv, u = _proj_in(h, w_in[l].astype(BF16), tm_in)
        attn_p = _attention(qkv, bias, lay, group=0)
        attn_s = _attention(qkv, bias, lay, group=1)
        convn = _conv(u, conv_w[l], conv_b[l], conv_ln_g[l], conv_ln_b[l], conv_out_g[l], lay, ts_conv)
        wr = jnp.zeros((D_MODEL, ROUTE_LANES), F32).at[:, :n_route].set(
            jnp.concatenate([rg_w[l], re_w[l]], axis=1)).astype(BF16)
        br = jnp.zeros((1, ROUTE_LANES), F32).at[0, :n_route].set(jnp.concatenate([rg_b[l], re_b[l]]))
        x_mid, h2, ri, rf = _proj_out(attn_p, attn_s, convn, x, modr, attn_out_g[l], w_out[l].astype(BF16),
                                      norm2_g[l], wr, br, lay, tm_out)
        row_tok, pos, block_expert, n_valid, run_info = _dispatch(ri[:, :TOP_K], tm_moe)
        xs = h2.at[row_tok].get(mode='promise_in_bounds')
        act = _gate_up(xs, e_gate, e_up, block_expert, n_valid, run_info, l, tm_moe)
        ys = _down(act, e_down, block_expert, n_valid, run_info, l, tm_moe)
        y0 = ys.at[pos[:, 0]].get(mode='promise_in_bounds')
        y1 = ys.at[pos[:, 1]].get(mode='promise_in_bounds')
        if l < DEPTH - 1:
            x, h = _combine(x_mid, y0, y1, rf, modr, modrs[l + 1], norm1_g[l + 1], lay, tm_res)
        else:
            y_prompt, y_sample = _final(x_mid, y0, y1, rf, modr, final_g, lay, tm_res)
    return (y_prompt.reshape(batch, seq, D_MODEL), y_sample.reshape(dec_batch, dec_seq, D_MODEL))


def kernel(x_prompt, x_sample, c_prompt, c_sample, ada_w, ada_b, norm1_g, w_in, conv_w, conv_b, conv_ln_g, conv_ln_b, attn_out_g, conv_out_g, w_out, norm2_g, rg_w, rg_b, re_w, re_b, e_gate, e_up, e_down, final_g):
    batch, seq, _ = x_prompt.shape
    dec_batch, dec_seq, _ = x_sample.shape
    lay = (batch, seq, dec_batch, dec_seq)
    c_all = jnp.concatenate([c_prompt, c_sample], axis=0)
    return _trunk(x_prompt.reshape(batch * seq, D_MODEL), x_sample.reshape(dec_batch * dec_seq, D_MODEL),
                  c_all, lay, ada_w, ada_b, norm1_g, w_in, conv_w, conv_b, conv_ln_g, conv_ln_b,
                  attn_out_g, conv_out_g, w_out, norm2_g, rg_w, rg_b, re_w, re_b,
                  e_gate, e_up, e_down, final_g)
```

```python
import functools
import math

import jax
import jax.numpy as jnp
from jax import lax
from jax.experimental import pallas as pl
from jax.experimental.pallas import tpu as pltpu

D_MODEL = 2048
DEPTH = 4
N_HEADS = 8
HEAD_DIM = 128
ATTN_WIDTH = N_HEADS * HEAD_DIM
CONV_CH = D_MODEL - ATTN_WIDTH
CONV_WIDTH = 31
DILATED_CONFIGS = ((128, 1), (512, 4), (2048, 16))
N_GROUPS = 4
EXPERTS_PER_GROUP = 8
N_EXPERTS = N_GROUPS * EXPERTS_PER_GROUP
TOP_K = 2
D_EXPERT = 1024
NORM_EPS = 1e-6
MASK_VALUE = -1e30

LANES = 128
ATT_TQ = 128
ATT_TK = 256
ATT_HALF = 64
ATT_STAGE = 256
ATT_UNROLL = 8
ATT_SPLIT = 4
CONV_HALO = 16
ROUTE_LANES = 128
PROJ_OUT_SUB = 256
VMEM_LIMIT = 56 * 1024 * 1024

F32 = jnp.float32
BF16 = jnp.bfloat16


def _params(sem, vmem=VMEM_LIMIT):
    return pltpu.CompilerParams(dimension_semantics=sem, vmem_limit_bytes=vmem)


def _seq_of_row(row, lay):
    batch, seq, _, dec_seq = lay
    n_prompt = batch * seq
    return jnp.where(row < n_prompt, row // seq, batch + (row - n_prompt) // dec_seq)


def _ada_body(c_ref, w_ref, b_ref, o_ref):
    c = c_ref[...]
    sc = c * jax.nn.sigmoid(c)
    o_ref[0] = jnp.dot(sc, w_ref[0], preferred_element_type=F32) + b_ref[0]


def _ada_mod(c_pad, ada_w, ada_b):
    ns8 = c_pad.shape[0]
    tn = 1024
    nj = 6 * D_MODEL // tn
    return pl.pallas_call(
        _ada_body,
        grid=(DEPTH, nj),
        in_specs=[
            pl.BlockSpec((ns8, D_MODEL), lambda l, j: (0, 0)),
            pl.BlockSpec((1, D_MODEL, tn), lambda l, j: (l, 0, j)),
            pl.BlockSpec((1, 1, tn), lambda l, j: (l, 0, j)),
        ],
        out_specs=pl.BlockSpec((1, ns8, tn), lambda l, j: (l, 0, j)),
        out_shape=jax.ShapeDtypeStruct((DEPTH, ns8, 6 * D_MODEL), F32),
        compiler_params=_params(("arbitrary", "arbitrary")),
    )(c_pad, ada_w, ada_b.reshape(DEPTH, 1, 6 * D_MODEL))


def _pack_rows(v):
    bits = pltpu.bitcast(v.astype(BF16).astype(F32), jnp.uint32)
    half = v.shape[1] // 2
    return (bits[:, :half] >> 16) | (bits[:, half:] & jnp.uint32(0xFFFF0000))


def _unpack_rows(words):
    return (pltpu.bitcast(words << 16, F32), pltpu.bitcast(words & jnp.uint32(0xFFFF0000), F32))


def _norm_mod(x, g_ref, scale_ref, shift_ref):
    ms = jnp.mean(x * x, axis=-1, keepdims=True)
    y = x * lax.rsqrt(ms + NORM_EPS) * g_ref[...]
    return (y * (1.0 + scale_ref[0]) + shift_ref[0]).astype(BF16)


def _prologue_body(xp_ref, xs_ref, shift_ref, scale_ref, g_ref, x_ref, h_ref, *, n_prompt_blocks):
    x = jnp.where(pl.program_id(0) < n_prompt_blocks, xp_ref[...], xs_ref[...])
    x_ref[...] = x
    h_ref[...] = _norm_mod(x, g_ref, scale_ref, shift_ref)


def _prologue(x_prompt, x_sample, modr, norm_g, lay, tm):
    n_prompt, n_sample = x_prompt.shape[0], x_sample.shape[0]
    t = n_prompt + n_sample
    npb = n_prompt // tm
    blk = pl.BlockSpec((tm, D_MODEL), lambda i: (i, 0))
    return pl.pallas_call(
        functools.partial(_prologue_body, n_prompt_blocks=npb),
        grid=(t // tm,),
        in_specs=[
            pl.BlockSpec((tm, D_MODEL), lambda i: (jnp.minimum(i, npb - 1), 0)),
            pl.BlockSpec((tm, D_MODEL), lambda i: (jnp.maximum(i - npb, 0), 0)),
            pl.BlockSpec((1, 1, D_MODEL), lambda i: (_seq_of_row(i * tm, lay) * 6 + 0, 0, 0)),
            pl.BlockSpec((1, 1, D_MODEL), lambda i: (_seq_of_row(i * tm, lay) * 6 + 1, 0, 0)),
            pl.BlockSpec((1, D_MODEL), lambda i: (0, 0)),
        ],
        out_specs=[blk, blk],
        out_shape=[jax.ShapeDtypeStruct((t, D_MODEL), F32), jax.ShapeDtypeStruct((t, D_MODEL), BF16)],
        compiler_params=_params(("arbitrary",)),
    )(x_prompt, x_sample, modr, modr, norm_g.reshape(1, D_MODEL))


def _proj_in_body(h_ref, wqkv_ref, wga_ref, wgb_ref, qkv_ref, u_ref):
    j = pl.program_id(1)

    @pl.when(j < 3)
    def _():
        r = jnp.dot(h_ref[...], wqkv_ref[...], preferred_element_type=F32)
        r = r * jnp.where(j == 0, HEAD_DIM ** -0.5, 1.0).astype(F32)
        for hh in range(N_HEADS):
            qkv_ref[hh] = r[:, hh * HEAD_DIM:(hh + 1) * HEAD_DIM]

    @pl.when(j == 3)
    def _():
        h = h_ref[...]
        a = jnp.dot(h, wga_ref[...], preferred_element_type=F32)
        b = jnp.dot(h, wgb_ref[...], preferred_element_type=F32)
        u_ref[...] = a * jax.nn.sigmoid(b)


def _proj_in(h, w_in_bf, tm):
    t = h.shape[0]
    tn = ATTN_WIDTH
    assert CONV_CH == tn
    resident = dict(pipeline_mode=pl.Buffered(1))
    return pl.pallas_call(
        _proj_in_body,
        grid=(t // tm, 4),
        in_specs=[
            pl.BlockSpec((tm, D_MODEL), lambda i, j: (i, 0)),
            pl.BlockSpec((D_MODEL, tn), lambda i, j: (0, jnp.minimum(j, 2))),
            pl.BlockSpec((D_MODEL, tn), lambda i, j: (0, 3), **resident),
            pl.BlockSpec((D_MODEL, tn), lambda i, j: (0, 4), **resident),
        ],
        out_specs=[
            pl.BlockSpec((N_HEADS, tm, HEAD_DIM), lambda i, j: (jnp.minimum(j, 2), i, 0)),
            pl.BlockSpec((tm, CONV_CH), lambda i, j: (i, 0)),
        ],
        out_shape=[
            jax.ShapeDtypeStruct((3 * N_HEADS, t, HEAD_DIM), F32),
            jax.ShapeDtypeStruct((t, CONV_CH), F32),
        ],
        compiler_params=_params(("arbitrary", "arbitrary")),
    )(h, w_in_bf, w_in_bf, w_in_bf)


def _alibi_slopes():
    return jnp.exp2(-8.0 * jnp.arange(1, N_HEADS + 1, dtype=F32) / N_HEADS)


def _attn_bias():
    i = jnp.arange(ATT_TQ)[:, None]
    j = jnp.arange(ATT_TK)[None, :]
    out = []
    for _, d in DILATED_CONFIGS:
        for v in range(3):
            rel = jnp.abs(j - ATT_HALF * v - i)
            dist = (d * rel).astype(F32)
            b = -_alibi_slopes()[:, None, None] * dist[None]
            out.append(jnp.where((rel <= ATT_HALF)[None], b, MASK_VALUE))
    return jnp.stack(out, axis=1)


def _strided_rows(start, size, stride):
    return pl.ds(start, size, stride=stride) if stride > 1 else pl.ds(start, size)


def _stage_two_level(q_ref, k_ref, v_ref, tmp, qs, ks, vs, seq_len):
    n1 = seq_len // ATT_SPLIT
    n2 = n1 // ATT_SPLIT
    for a in range(ATT_SPLIT):
        def dense(ch, cr, a=a):
            src = pl.ds(a + ch * (ATT_STAGE * ATT_SPLIT), ATT_STAGE, stride=ATT_SPLIT)
            dst = pl.ds(pl.multiple_of(ch * ATT_STAGE, ATT_STAGE), ATT_STAGE)
            tmp[0, dst, :] = q_ref[0, src, :]
            tmp[1, dst, :] = k_ref[0, src, :]
            tmp[2, dst, :] = v_ref[0, src, :]
            return cr

        lax.fori_loop(0, n1 // ATT_STAGE, dense, 0, unroll=2)
        for b in range(ATT_SPLIT):
            def cast(ch, cr, a=a, b=b):
                src = pl.ds(b + ch * (ATT_STAGE * ATT_SPLIT), ATT_STAGE, stride=ATT_SPLIT)
                dst = pl.ds(pl.multiple_of((a + ATT_SPLIT * b) * n2 + ch * ATT_STAGE, ATT_STAGE), ATT_STAGE)
                qs[dst, :] = tmp[0, src, :].astype(BF16)
                ks[dst, :] = tmp[1, src, :].astype(BF16)
                vs[dst, :] = tmp[2, src, :].astype(BF16)
                return cr

            lax.fori_loop(0, n2 // ATT_STAGE, cast, 0)


def _attn_body(q_ref, k_ref, v_ref, bias_ref, *rest, seq_len):
    o_ref, lse_scr, qs, ks, vs, tmp = rest
    order = sorted(range(len(DILATED_CONFIGS)), key=lambda c: -DILATED_CONFIGS[c][1])
    for step, c in enumerate(order):
        d = DILATED_CONFIGS[c][1]
        sub_len = seq_len // d
        first, last = step == 0, step == len(order) - 1

        nstage = sub_len // ATT_STAGE
        nt = sub_len // ATT_TQ

        def stage(n, cr, d=d, sub_len=sub_len, nstage=nstage):
            r = lax.div(n, nstage)
            ch = lax.rem(n, nstage)
            dst = pl.ds(pl.multiple_of(n * ATT_STAGE, ATT_STAGE), ATT_STAGE)
            src = _strided_rows(r + ch * (ATT_STAGE * d), ATT_STAGE, d)
            qs[dst, :] = q_ref[0, src, :].astype(BF16)
            ks[dst, :] = k_ref[0, src, :].astype(BF16)
            vs[dst, :] = v_ref[0, src, :].astype(BF16)
            return cr

        if d == ATT_SPLIT * ATT_SPLIT:
            _stage_two_level(q_ref, k_ref, v_ref, tmp, qs, ks, vs, seq_len)
        else:
            lax.fori_loop(0, d * nstage, stage, 0, unroll=2)

        def tile(n, ct, c=c, d=d, sub_len=sub_len, nt=nt, first=first, last=last):
            r = lax.div(n, nt)
            l0 = lax.rem(n, nt) * ATT_TQ
            k0 = jnp.clip(l0 - ATT_HALF, 0, sub_len - ATT_TK)
            var = lax.div(l0 - k0, ATT_HALF)
            base = r * sub_len
            q = qs[pl.ds(pl.multiple_of(base + l0, ATT_TQ), ATT_TQ), :]
            k = ks[pl.ds(pl.multiple_of(base + k0, ATT_HALF), ATT_TK), :]
            v = vs[pl.ds(pl.multiple_of(base + k0, ATT_HALF), ATT_TK), :]
            s = lax.dot_general(q, k, (((1,), (1,)), ((), ())), preferred_element_type=F32)
            s = s + bias_ref[0, 3 * c + var]
            m = jnp.max(s, axis=-1, keepdims=True)
            p = jnp.exp(s - m)
            l = jnp.sum(p, axis=-1, keepdims=True)
            o = jnp.dot(p.astype(BF16), v, preferred_element_type=F32)
            mb = jnp.broadcast_to(m, (ATT_TQ, LANES))
            lb = jnp.broadcast_to(l, (ATT_TQ, LANES))
            rows = _strided_rows(r + l0 * d, ATT_TQ, d)
            if first:
                o_ref[rows, :] = o / lb
                lse_scr[rows, :] = mb + jnp.log(lb)
            else:
                la = lse_scr[rows, :]
                mx = jnp.maximum(la, mb)
                wa = jnp.exp(la - mx)
                wb = jnp.exp(mb - mx)
                den = wa + wb * lb
                o_ref[rows, :] = (wa * o_ref[rows, :] + wb * o) / den
                if not last:
                    lse_scr[rows, :] = mx + jnp.log(den)
            return ct

        lax.fori_loop(0, d * nt, tile, 0, unroll=ATT_UNROLL)


def _attention(qkv, bias, lay, *, group):
    batch, seq, dec_batch, dec_seq = lay
    if group == 0:
        nb, s_len, blk0 = batch, seq, 0
    else:
        assert (batch * seq) % dec_seq == 0
        nb, s_len, blk0 = dec_batch, dec_seq, batch * seq // dec_seq
    assert s_len % (ATT_TK * DILATED_CONFIGS[-1][1]) == 0

    def hm(part):
        return lambda b, h: (part * N_HEADS + h, blk0 + b, 0)

    return pl.pallas_call(
        functools.partial(_attn_body, seq_len=s_len),
        grid=(nb, N_HEADS),
        in_specs=[
            pl.BlockSpec((1, s_len, HEAD_DIM), hm(0)),
            pl.BlockSpec((1, s_len, HEAD_DIM), hm(1)),
            pl.BlockSpec((1, s_len, HEAD_DIM), hm(2)),
            pl.BlockSpec((1, bias.shape[1], ATT_TQ, ATT_TK), lambda b, h: (h, 0, 0, 0)),
        ],
        out_specs=pl.BlockSpec((s_len, HEAD_DIM), lambda b, h: (b, h)),
        out_shape=jax.ShapeDtypeStruct((nb * s_len, ATTN_WIDTH), F32),
        scratch_shapes=([pltpu.VMEM((s_len, LANES), F32)] + [pltpu.VMEM((s_len, HEAD_DIM), BF16)] * 3
                        + [pltpu.VMEM((3, s_len // ATT_SPLIT, HEAD_DIM), F32)]),
        compiler_params=_params(("arbitrary", "arbitrary")),
    )(qkv, qkv, qkv, bias)


def _conv_body(prev_ref, cur_ref, next_ref, w_ref, cb_ref, lg_ref, lb_ref, og_ref, o_ref,
               pad_scr, cv_scr, *, lay, ts):
    batch, seq, dec_batch, dec_seq = lay
    n_prompt = batch * seq
    row0 = pl.program_id(0) * ts
    row1 = row0 + ts
    pos0 = jnp.where(row0 < n_prompt, row0 % seq, (row0 - n_prompt) % dec_seq)
    pos1 = jnp.where(row1 <= n_prompt, row1 % seq, (row1 - n_prompt) % dec_seq)
    nch = CONV_CH // LANES
    lanes = [slice(cc * LANES, (cc + 1) * LANES) for cc in range(nch)]
    for cc in range(nch):
        pad_scr[cc, 0:CONV_HALO, :] = jnp.where(pos0 != 0, prev_ref[:, lanes[cc]], 0.0)
        pad_scr[cc, CONV_HALO:CONV_HALO + ts, :] = cur_ref[:, lanes[cc]]
        pad_scr[cc, CONV_HALO + ts:, :] = jnp.where(pos1 != 0, next_ref[:, lanes[cc]], 0.0)

    rc = 128
    base = CONV_HALO - CONV_WIDTH // 2

    def lane_chunk(cc, carry):
        for r in range(ts // rc):
            for par in range(2):
                acc = jnp.zeros((rc // 2, LANES), F32)
                for j in range(CONV_WIDTH):
                    acc = acc + (pad_scr[cc, pl.ds(base + r * rc + par + j, rc // 2, stride=2), :]
                                 * w_ref[cc, pl.ds(j, 1), :])
                cv_scr[cc, pl.ds(r * rc + par, rc // 2, stride=2), :] = acc
        return carry

    lax.fori_loop(0, nch, lane_chunk, 0)

    rn = 32
    inv_c = 1.0 / CONV_CH

    def row_chunk(r, carry):
        r0 = pl.multiple_of(r * rn, rn)
        u = [cv_scr[cc, pl.ds(r0, rn), :] + cb_ref[:, lanes[cc]] for cc in range(nch)]
        mu = jnp.sum(sum(u), axis=-1, keepdims=True) * inv_c
        u = [a - mu for a in u]
        var = jnp.sum(sum(a * a for a in u), axis=-1, keepdims=True) * inv_c
        rstd = lax.rsqrt(var + NORM_EPS)
        y = [a * rstd * lg_ref[:, lanes[cc]] + lb_ref[:, lanes[cc]] for cc, a in enumerate(u)]
        y = [a * jax.nn.sigmoid(a) for a in y]
        ms = jnp.sum(sum(a * a for a in y), axis=-1, keepdims=True) * inv_c
        rr = lax.rsqrt(ms + NORM_EPS)
        for cc, a in enumerate(y):
            o_ref[pl.ds(r0, rn), lanes[cc]] = (a * rr * og_ref[:, lanes[cc]]).astype(o_ref.dtype)
        return carry

    lax.fori_loop(0, ts // rn, row_chunk, 0, unroll=4)


def _conv(u, conv_w, conv_b, ln_g, ln_b, out_g, lay, ts):
    t = u.shape[0]
    hb = ts // CONV_HALO
    nhb = t // CONV_HALO
    nch = CONV_CH // LANES
    w_slabs = conv_w.reshape(CONV_WIDTH, nch, LANES).transpose(1, 0, 2)
    row = lambda a: a.reshape(1, CONV_CH)
    return pl.pallas_call(
        functools.partial(_conv_body, lay=lay, ts=ts),
        grid=(t // ts,),
        in_specs=[
            pl.BlockSpec((CONV_HALO, CONV_CH), lambda i: (jnp.maximum(i * hb - 1, 0), 0)),
            pl.BlockSpec((ts, CONV_CH), lambda i: (i, 0)),
            pl.BlockSpec((CONV_HALO, CONV_CH), lambda i: (jnp.minimum((i + 1) * hb, nhb - 1), 0)),
            pl.BlockSpec((nch, CONV_WIDTH, LANES), lambda i: (0, 0, 0)),
            pl.BlockSpec((1, CONV_CH), lambda i: (0, 0)),
            pl.BlockSpec((1, CONV_CH), lambda i: (0, 0)),
            pl.BlockSpec((1, CONV_CH), lambda i: (0, 0)),
            pl.BlockSpec((1, CONV_CH), lambda i: (0, 0)),
        ],
        out_specs=pl.BlockSpec((ts, CONV_CH), lambda i: (i, 0)),
        out_shape=jax.ShapeDtypeStruct((t, CONV_CH), BF16),
        scratch_shapes=[pltpu.VMEM((nch, ts + 2 * CONV_HALO, LANES), F32),
                        pltpu.VMEM((nch, ts, LANES), F32)],
        compiler_params=_params(("arbitrary",)),
    )(u, u, u, w_slabs, row(conv_b), row(ln_g), row(ln_b), row(out_g))


def _proj_out_body(attn_p_ref, attn_s_ref, conv_ref, x_ref, ag_ref, wout_ref, gate_ref, g2_ref, shift_ref,
                   scale_ref, wr_ref, br_ref, xmid_ref, h_ref, ri_ref, rf_ref, *, n_prompt_blocks):
    for s in range(attn_p_ref.shape[0] // PROJ_OUT_SUB):
        _proj_out_rows(slice(s * PROJ_OUT_SUB, (s + 1) * PROJ_OUT_SUB), attn_p_ref, attn_s_ref, conv_ref,
                       x_ref, ag_ref, wout_ref, gate_ref, g2_ref, shift_ref, scale_ref, wr_ref, br_ref,
                       xmid_ref, h_ref, ri_ref, rf_ref, n_prompt_blocks)


def _proj_out_rows(rs, attn_p_ref, attn_s_ref, conv_ref, x_ref, ag_ref, wout_ref, gate_ref, g2_ref, shift_ref,
                   scale_ref, wr_ref, br_ref, xmid_ref, h_ref, ri_ref, rf_ref, n_prompt_blocks):
    a = jnp.where(pl.program_id(0) < n_prompt_blocks, attn_p_ref[rs, :], attn_s_ref[rs, :])
    ms = jnp.mean(a * a, axis=-1, keepdims=True)
    an = (a * lax.rsqrt(ms + NORM_EPS) * ag_ref[...]).astype(BF16)
    o = jnp.dot(an, wout_ref[0:ATTN_WIDTH, :], preferred_element_type=F32)
    o = o + jnp.dot(conv_ref[rs, :], wout_ref[ATTN_WIDTH:, :], preferred_element_type=F32)
    x = x_ref[rs, :] + gate_ref[0] * o
    xmid_ref[rs, :] = x
    hb = _norm_mod(x, g2_ref, scale_ref, shift_ref)
    h_ref[rs, :] = _pack_rows(hb)

    lg = jnp.dot(hb, wr_ref[...], preferred_element_type=F32) + br_ref[...]
    lane = lax.broadcasted_iota(jnp.int32, lg.shape, 1)
    lane_f = lane.astype(F32)
    neg = jnp.float32(-jnp.inf)
    big = jnp.float32(ROUTE_LANES)
    gmask = lane < N_GROUPS
    lgm = jnp.where(gmask, lg, neg)
    gmax = jnp.max(lgm, axis=-1, keepdims=True)
    gidx = jnp.min(jnp.where(lgm == gmax, lane_f, big), axis=-1, keepdims=True).astype(jnp.int32)
    gsum = jnp.sum(jnp.where(gmask, jnp.exp(lg - gmax), 0.0), axis=-1, keepdims=True)
    pg = 1.0 / gsum
    lo = N_GROUPS + gidx * EXPERTS_PER_GROUP
    emask = (lane >= lo) & (lane < lo + EXPERTS_PER_GROUP)
    le1 = jnp.where(emask, lg, neg)
    e1 = jnp.max(le1, axis=-1, keepdims=True)
    i1 = jnp.min(jnp.where(le1 == e1, lane_f, big), axis=-1, keepdims=True)
    le2 = jnp.where(lane_f == i1, neg, le1)
    e2 = jnp.max(le2, axis=-1, keepdims=True)
    i2 = jnp.min(jnp.where(le2 == e2, lane_f, big), axis=-1, keepdims=True)
    r = jnp.exp(e2 - e1)
    w1 = pg / (1.0 + r)
    w2 = pg * r / (1.0 + r)
    id1 = (i1 - N_GROUPS).astype(jnp.int32)
    id2 = (i2 - N_GROUPS).astype(jnp.int32)
    ri_ref[rs, :] = jnp.where(lane == 0, id1, jnp.where(lane == 1, id2, 0))
    rf_ref[rs, :] = jnp.where(lane == 0, w1, jnp.where(lane == 1, w2, 0.0))


def _proj_out(attn_p, attn_s, convn, x, modr, attn_g, w_out_bf, norm2_g, wr, br, lay, tm):
    t = x.shape[0]
    npb = attn_p.shape[0] // tm

    def seq(i):
        return _seq_of_row(i * tm, lay)

    row = lambda a, n: a.reshape(1, n)
    return pl.pallas_call(
        functools.partial(_proj_out_body, n_prompt_blocks=npb),
        grid=(t // tm,),
        in_specs=[
            pl.BlockSpec((tm, ATTN_WIDTH), lambda i: (jnp.minimum(i, npb - 1), 0)),
            pl.BlockSpec((tm, ATTN_WIDTH), lambda i: (jnp.maximum(i - npb, 0), 0)),
            pl.BlockSpec((tm, CONV_CH), lambda i: (i, 0)),
            pl.BlockSpec((tm, D_MODEL), lambda i: (i, 0)),
            pl.BlockSpec((1, ATTN_WIDTH), lambda i: (0, 0)),
            pl.BlockSpec((D_MODEL, D_MODEL), lambda i: (0, 0), pipeline_mode=pl.Buffered(1)),
            pl.BlockSpec((1, 1, D_MODEL), lambda i: (seq(i) * 6 + 2, 0, 0)),
            pl.BlockSpec((1, D_MODEL), lambda i: (0, 0)),
            pl.BlockSpec((1, 1, D_MODEL), lambda i: (seq(i) * 6 + 3, 0, 0)),
            pl.BlockSpec((1, 1, D_MODEL), lambda i: (seq(i) * 6 + 4, 0, 0)),
            pl.BlockSpec((D_MODEL, ROUTE_LANES), lambda i: (0, 0)),
            pl.BlockSpec((1, ROUTE_LANES), lambda i: (0, 0)),
        ],
        out_specs=[
            pl.BlockSpec((tm, D_MODEL), lambda i: (i, 0)),
            pl.BlockSpec((tm, D_MODEL // 2), lambda i: (i, 0)),
            pl.BlockSpec((tm, ROUTE_LANES), lambda i: (i, 0)),
            pl.BlockSpec((tm, ROUTE_LANES), lambda i: (i, 0)),
        ],
        out_shape=[
            jax.ShapeDtypeStruct((t, D_MODEL), F32),
            jax.ShapeDtypeStruct((t, D_MODEL // 2), jnp.uint32),
            jax.ShapeDtypeStruct((t, ROUTE_LANES), jnp.int32),
            jax.ShapeDtypeStruct((t, ROUTE_LANES), F32),
        ],
        compiler_params=_params(("arbitrary",)),
    )(attn_p, attn_s, convn, x, row(attn_g, ATTN_WIDTH), w_out_bf, modr, row(norm2_g, D_MODEL), modr, modr,
      wr, br)


def _dispatch(experts, tm):
    t = experts.shape[0]
    a = t * TOP_K
    nb = -(-(a + N_EXPERTS * (tm - 1)) // tm)
    flat_e = experts.reshape(-1)
    order = jnp.argsort(flat_e).astype(jnp.int32)
    rank = jnp.argsort(order).astype(jnp.int32)
    counts = jnp.sum(flat_e[:, None] == jnp.arange(N_EXPERTS)[None, :], axis=0).astype(jnp.int32)
    starts = jnp.cumsum(counts) - counts
    padded = ((counts + tm - 1) // tm) * tm
    padded_ends = jnp.cumsum(padded)
    padded_starts = padded_ends - padded
    shift = (padded_starts - starts).astype(jnp.int32)
    pos = (rank + shift[flat_e]).reshape(t, TOP_K)
    block_expert = jnp.minimum(
        jnp.sum(padded_ends[None, :] <= (jnp.arange(nb) * tm)[:, None], axis=1), N_EXPERTS - 1).astype(jnp.int32)
    row = jnp.arange(nb * tm, dtype=jnp.int32)
    row_e = jnp.repeat(block_expert, tm)
    sorted_pos = row - shift[row_e]
    valid = sorted_pos < (starts + counts)[row_e]
    src = order[jnp.clip(sorted_pos, 0, a - 1)]
    row_tok = jnp.where(valid, src // TOP_K, row % t).astype(jnp.int32)
    n_valid = (padded_ends[-1] // tm).astype(jnp.int32).reshape(1)
    nonempty = counts > 0
    slot = ((jnp.cumsum(nonempty) - nonempty) % 2).astype(jnp.int32)
    ids = jnp.where(nonempty, jnp.arange(N_EXPERTS), N_EXPERTS)
    nxt = lax.cummin(jnp.concatenate([ids[1:], jnp.array([N_EXPERTS])]), reverse=True)
    nxt = jnp.where(nxt >= N_EXPERTS, -1, nxt).astype(jnp.int32)
    run_info = jnp.concatenate([slot, nxt, (padded_starts // tm).astype(jnp.int32)])
    return row_tok, pos, block_expert, n_valid, run_info


def _expert_weights(be_ref, ri_ref, hbm_refs, bufs, sem, layer, tn):
    j, i = pl.program_id(0), pl.program_id(1)
    e = be_ref[i]
    slot, nxt, first_blk = ri_ref[e], ri_ref[N_EXPERTS + e], ri_ref[2 * N_EXPERTS + e]

    def copies(expert, s):
        return [pltpu.make_async_copy(h.at[layer, expert, :, pl.ds(pl.multiple_of(j * tn, tn), tn)], b.at[s],
                                      sem.at[n, s])
                for n, (h, b) in enumerate(zip(hbm_refs, bufs))]

    @pl.when(i == first_blk)
    def _():
        @pl.when(i == 0)
        def _():
            for c in copies(e, slot):
                c.start()

        for c in copies(e, slot):
            c.wait()

        @pl.when(nxt >= 0)
        def _():
            for c in copies(nxt, 1 - slot):
                c.start()

    return slot


def _gate_up_body(be_ref, nv_ref, ri_ref, xs_ref, wg_hbm, wu_hbm, a_ref, wg_buf, wu_buf, sem, *, layer, tn):
    i = pl.program_id(1)
    half = D_MODEL // 2

    @pl.when(i < nv_ref[0])
    def _():
        slot = _expert_weights(be_ref, ri_ref, (wg_hbm, wu_hbm), (wg_buf, wu_buf), sem, layer, tn)
        lo, hi = (v.astype(BF16) for v in _unpack_rows(xs_ref[...]))
        g = (jnp.dot(lo, wg_buf[slot, :half, :].astype(BF16), preferred_element_type=F32)
             + jnp.dot(hi, wg_buf[slot, half:, :].astype(BF16), preferred_element_type=F32))
        u = (jnp.dot(lo, wu_buf[slot, :half, :].astype(BF16), preferred_element_type=F32)
             + jnp.dot(hi, wu_buf[slot, half:, :].astype(BF16), preferred_element_type=F32))
        a_ref[...] = (g * jax.nn.sigmoid(g) * u).astype(a_ref.dtype)

    @pl.when(i >= nv_ref[0])
    def _():
        a_ref[...] = jnp.zeros_like(a_ref)


def _gate_up(xs, e_gate, e_up, block_expert, n_valid, run_info, layer, tm):
    p = xs.shape[0]
    tn = D_EXPERT
    return pl.pallas_call(
        functools.partial(_gate_up_body, layer=layer, tn=tn),
        grid_spec=pltpu.PrefetchScalarGridSpec(
            num_scalar_prefetch=3,
            grid=(D_EXPERT // tn, p // tm),
            in_specs=[
                pl.BlockSpec((tm, D_MODEL // 2), lambda j, i, be, nv, ri: (jnp.minimum(i, nv[0] - 1), 0)),
                pl.BlockSpec(memory_space=pl.ANY),
                pl.BlockSpec(memory_space=pl.ANY),
            ],
            out_specs=pl.BlockSpec((tm, tn), lambda j, i, be, nv, ri: (i, j)),
            scratch_shapes=[pltpu.VMEM((2, D_MODEL, tn), F32), pltpu.VMEM((2, D_MODEL, tn), F32),
                            pltpu.SemaphoreType.DMA((2, 2))],
        ),
        out_shape=jax.ShapeDtypeStruct((p, D_EXPERT), BF16),
        compiler_params=_params(("arbitrary", "arbitrary")),
    )(block_expert, n_valid, run_info, xs, e_gate, e_up)


def _down_body(be_ref, nv_ref, ri_ref, a_ref, wd_hbm, y_ref, wd_buf, sem, *, layer, tn):
    i = pl.program_id(1)

    @pl.when(i < nv_ref[0])
    def _():
        slot = _expert_weights(be_ref, ri_ref, (wd_hbm,), (wd_buf,), sem, layer, tn)
        y_ref[...] = _pack_rows(jnp.dot(a_ref[...], wd_buf[slot].astype(BF16), preferred_element_type=F32))

    @pl.when(i >= nv_ref[0])
    def _():
        y_ref[...] = jnp.zeros_like(y_ref)


def _down(a, e_down, block_expert, n_valid, run_info, layer, tm):
    p = a.shape[0]
    tn = D_MODEL
    return pl.pallas_call(
        functools.partial(_down_body, layer=layer, tn=tn),
        grid_spec=pltpu.PrefetchScalarGridSpec(
            num_scalar_prefetch=3,
            grid=(D_MODEL // tn, p // tm),
            in_specs=[
                pl.BlockSpec((tm, D_EXPERT), lambda j, i, be, nv, ri: (jnp.minimum(i, nv[0] - 1), 0)),
                pl.BlockSpec(memory_space=pl.ANY),
            ],
            out_specs=pl.BlockSpec((tm, tn // 2), lambda j, i, be, nv, ri: (i, j)),
            scratch_shapes=[pltpu.VMEM((2, D_EXPERT, tn), F32), pltpu.SemaphoreType.DMA((1, 2))],
        ),
        out_shape=jax.ShapeDtypeStruct((p, D_MODEL // 2), jnp.uint32),
        compiler_params=_params(("arbitrary", "arbitrary")),
    )(block_expert, n_valid, run_info, a, e_down)


def _moe_residual(x_ref, y0_ref, y1_ref, rf_ref, gate_ref):
    rf = rf_ref[...]
    y0 = jnp.concatenate(_unpack_rows(y0_ref[...]), axis=1)
    y1 = jnp.concatenate(_unpack_rows(y1_ref[...]), axis=1)
    return x_ref[...] + gate_ref[0] * (rf[:, 0:1] * y0 + rf[:, 1:2] * y1)


def _combine_body(x_ref, y0_ref, y1_ref, rf_ref, gate_ref, shift_ref, scale_ref, g_ref, o_ref, h_ref):
    x = _moe_residual(x_ref, y0_ref, y1_ref, rf_ref, gate_ref)
    o_ref[...] = x
    h_ref[...] = _norm_mod(x, g_ref, scale_ref, shift_ref)


def _combine(x, y0, y1, rf, modr, modr_next, norm_g_next, lay, tm):
    t = x.shape[0]
    blk = pl.BlockSpec((tm, D_MODEL), lambda i: (i, 0))
    yblk = pl.BlockSpec((tm, D_MODEL // 2), lambda i: (i, 0))

    def mod_row(k):
        return pl.BlockSpec((1, 1, D_MODEL), lambda i: (_seq_of_row(i * tm, lay) * 6 + k, 0, 0))

    return pl.pallas_call(
        _combine_body,
        grid=(t // tm,),
        in_specs=[blk, yblk, yblk, pl.BlockSpec((tm, ROUTE_LANES), lambda i: (i, 0)),
                  mod_row(5), mod_row(0), mod_row(1), pl.BlockSpec((1, D_MODEL), lambda i: (0, 0))],
        out_specs=[blk, blk],
        out_shape=[jax.ShapeDtypeStruct((t, D_MODEL), F32), jax.ShapeDtypeStruct((t, D_MODEL), BF16)],
        compiler_params=_params(("arbitrary",)),
    )(x, y0, y1, rf, modr, modr_next, modr_next, norm_g_next.reshape(1, D_MODEL))


def _final_body(x_ref, y0_ref, y1_ref, rf_ref, gate_ref, g_ref, op_ref, os_ref, *, n_prompt_blocks):
    i = pl.program_id(0)
    x = _moe_residual(x_ref, y0_ref, y1_ref, rf_ref, gate_ref)
    ms = jnp.mean(x * x, axis=-1, keepdims=True)
    y = x * lax.rsqrt(ms + NORM_EPS) * g_ref[...]

    @pl.when(i < n_prompt_blocks)
    def _():
        op_ref[...] = y

    @pl.when(i >= n_prompt_blocks)
    def _():
        os_ref[...] = y


def _final(x, y0, y1, rf, modr, final_g, lay, tm):
    batch, seq, dec_batch, dec_seq = lay
    t = x.shape[0]
    n_prompt = batch * seq
    npb = n_prompt // tm
    blk = pl.BlockSpec((tm, D_MODEL), lambda i: (i, 0))
    yblk = pl.BlockSpec((tm, D_MODEL // 2), lambda i: (i, 0))
    return pl.pallas_call(
        functools.partial(_final_body, n_prompt_blocks=npb),
        grid=(t // tm,),
        in_specs=[blk, yblk, yblk, pl.BlockSpec((tm, ROUTE_LANES), lambda i: (i, 0)),
                  pl.BlockSpec((1, 1, D_MODEL), lambda i: (_seq_of_row(i * tm, lay) * 6 + 5, 0, 0)),
                  pl.BlockSpec((1, D_MODEL), lambda i: (0, 0))],
        out_specs=[
            pl.BlockSpec((tm, D_MODEL), lambda i: (jnp.minimum(i, npb - 1), 0)),
            pl.BlockSpec((tm, D_MODEL), lambda i: (jnp.maximum(i - npb, 0), 0)),
        ],
        out_shape=[
            jax.ShapeDtypeStruct((n_prompt, D_MODEL), F32),
            jax.ShapeDtypeStruct((t - n_prompt, D_MODEL), F32),
        ],
        compiler_params=_params(("arbitrary",)),
    )(x, y0, y1, rf, modr, final_g.reshape(1, D_MODEL))


def _trunk(x_prompt, x_sample, c_all, lay, ada_w, ada_b, norm1_g, w_in, conv_w, conv_b, conv_ln_g, conv_ln_b,
           attn_out_g, conv_out_g, w_out, norm2_g, rg_w, rg_b, re_w, re_b, e_gate, e_up, e_down,
           final_g, *, tm_in=1024, tm_out=512, ts_conv=512, tm_moe=512, tm_res=512):
    batch, seq, dec_batch, dec_seq = lay
    ns = c_all.shape[0]
    ns8 = -(-ns // 8) * 8
    c_pad = jnp.zeros((ns8, D_MODEL), F32).at[:ns].set(c_all)
    mod = _ada_mod(c_pad, ada_w, ada_b)
    modrs = [mod[l, :ns].reshape(ns * 6, 1, D_MODEL) for l in range(DEPTH)]
    bias = _attn_bias()
    n_route = N_GROUPS + N_EXPERTS
    y_prompt = y_sample = None
    x, h = _prologue(x_prompt, x_sample, modrs[0], norm1_g[0], lay, tm_res)
    for l in range(DEPTH):
        modr = modrs[l]
        qkv, u = _proj_in(h, w_in[l].astype(BF16), tm_in)
        attn_p = _attention(qkv, bias, lay, group=0)
        attn_s = _attention(qkv, bias, lay, group=1)
        convn = _conv(u, conv_w[l], conv_b[l], conv_ln_g[l], conv_ln_b[l], conv_out_g[l], lay, ts_conv)
        wr = jnp.zeros((D_MODEL, ROUTE_LANES), F32).at[:, :n_route].set(
            jnp.concatenate([rg_w[l], re_w[l]], axis=1)).astype(BF16)
        br = jnp.zeros((1, ROUTE_LANES), F32).at[0, :n_route].set(jnp.concatenate([rg_b[l], re_b[l]]))
        x_mid, h2, ri, rf = _proj_out(attn_p, attn_s, convn, x, modr, attn_out_g[l], w_out[l].astype(BF16),
                                      norm2_g[l], wr, br, lay, tm_out)
        row_tok, pos, block_expert, n_valid, run_info = _dispatch(ri[:, :TOP_K], tm_moe)
        xs = h2.at[row_tok].get(mode='promise_in_bounds')
        act = _gate_up(xs, e_gate, e_up, block_expert, n_valid, run_info, l, tm_moe)
        ys = _down(act, e_down, block_expert, n_valid, run_info, l, tm_moe)
        y0 = ys.at[pos[:, 0]].get(mode='promise_in_bounds')
        y1 = ys.at[pos[:, 1]].get(mode='promise_in_bounds')
        if l < DEPTH - 1:
            x, h = _combine(x_mid, y0, y1, rf, modr, modrs[l + 1], norm1_g[l + 1], lay, tm_res)
        else:
            y_prompt, y_sample = _final(x_mid, y0, y1, rf, modr, final_g, lay, tm_res)
    return (y_prompt.reshape(batch, seq, D_MODEL), y_sample.reshape(dec_batch, dec_seq, D_MODEL))


def kernel(x_prompt, x_sample, c_prompt, c_sample, ada_w, ada_b, norm1_g, w_in, conv_w, conv_b, conv_ln_g, conv_ln_b, attn_out_g, conv_out_g, w_out, norm2_g, rg_w, rg_b, re_w, re_b, e_gate, e_up, e_down, final_g):
    batch, seq, _ = x_prompt.shape
    dec_batch, dec_seq, _ = x_sample.shape
    lay = (batch, seq, dec_batch, dec_seq)
    c_all = jnp.concatenate([c_prompt, c_sample], axis=0)
    return _trunk(x_prompt.reshape(batch * seq, D_MODEL), x_sample.reshape(dec_batch * dec_seq, D_MODEL),
                  c_all, lay, ada_w, ada_b, norm1_g, w_in, conv_w, conv_b, conv_ln_g, conv_ln_b,
                  attn_out_g, conv_out_g, w_out, norm2_g, rg_w, rg_b, re_w, re_b,
                  e_gate, e_up, e_down, final_g)
```

```python
import functools
import math

import jax
import jax.numpy as jnp
from jax import lax
from jax.experimental import pallas as pl
from jax.experimental.pallas import tpu as pltpu

D_MODEL = 2048
DEPTH = 4
N_HEADS = 8
HEAD_DIM = 128
ATTN_WIDTH = N_HEADS * HEAD_DIM
CONV_CH = D_MODEL - ATTN_WIDTH
CONV_WIDTH = 31
DILATED_CONFIGS = ((128, 1), (512, 4), (2048, 16))
N_GROUPS = 4
EXPERTS_PER_GROUP = 8
N_EXPERTS = N_GROUPS * EXPERTS_PER_GROUP
TOP_K = 2
D_EXPERT = 1024
NORM_EPS = 1e-6
MASK_VALUE = -1e30

LANES = 128
ATT_TQ = 128
ATT_TK = 256
ATT_HALF = 64
ATT_STAGE = 256
ATT_UNROLL = 32
ATT_SPLIT = 4
CONV_HALO = 16
ROUTE_LANES = 128
PROJ_OUT_SUB = 256
VMEM_LIMIT = 56 * 1024 * 1024

F32 = jnp.float32
BF16 = jnp.bfloat16


def _params(sem, vmem=VMEM_LIMIT):
    return pltpu.CompilerParams(dimension_semantics=sem, vmem_limit_bytes=vmem)


def _seq_of_row(row, lay):
    batch, seq, _, dec_seq = lay
    n_prompt = batch * seq
    return jnp.where(row < n_prompt, row // seq, batch + (row - n_prompt) // dec_seq)


def _ada_body(c_ref, w_ref, b_ref, o_ref):
    c = c_ref[...]
    sc = c * jax.nn.sigmoid(c)
    o_ref[0] = jnp.dot(sc, w_ref[0], preferred_element_type=F32) + b_ref[0]


def _ada_mod(c_pad, ada_w, ada_b):
    ns8 = c_pad.shape[0]
    tn = 1024
    nj = 6 * D_MODEL // tn
    return pl.pallas_call(
        _ada_body,
        grid=(DEPTH, nj),
        in_specs=[
            pl.BlockSpec((ns8, D_MODEL), lambda l, j: (0, 0)),
            pl.BlockSpec((1, D_MODEL, tn), lambda l, j: (l, 0, j)),
            pl.BlockSpec((1, 1, tn), lambda l, j: (l, 0, j)),
        ],
        out_specs=pl.BlockSpec((1, ns8, tn), lambda l, j: (l, 0, j)),
        out_shape=jax.ShapeDtypeStruct((DEPTH, ns8, 6 * D_MODEL), F32),
        compiler_params=_params(("arbitrary", "arbitrary")),
    )(c_pad, ada_w, ada_b.reshape(DEPTH, 1, 6 * D_MODEL))


def _pack_rows(v):
    bits = pltpu.bitcast(v.astype(BF16).astype(F32), jnp.uint32)
    half = v.shape[1] // 2
    return (bits[:, :half] >> 16) | (bits[:, half:] & jnp.uint32(0xFFFF0000))


def _unpack_rows(words):
    return (pltpu.bitcast(words << 16, F32), pltpu.bitcast(words & jnp.uint32(0xFFFF0000), F32))


def _norm_mod(x, g_ref, scale_ref, shift_ref):
    ms = jnp.mean(x * x, axis=-1, keepdims=True)
    y = x * lax.rsqrt(ms + NORM_EPS) * g_ref[...]
    return (y * (1.0 + scale_ref[0]) + shift_ref[0]).astype(BF16)


def _prologue_body(xp_ref, xs_ref, shift_ref, scale_ref, g_ref, x_ref, h_ref, *, n_prompt_blocks):
    x = jnp.where(pl.program_id(0) < n_prompt_blocks, xp_ref[...], xs_ref[...])
    x_ref[...] = x
    h_ref[...] = _norm_mod(x, g_ref, scale_ref, shift_ref)


def _prologue(x_prompt, x_sample, modr, norm_g, lay, tm):
    n_prompt, n_sample = x_prompt.shape[0], x_sample.shape[0]
    t = n_prompt + n_sample
    npb = n_prompt // tm
    blk = pl.BlockSpec((tm, D_MODEL), lambda i: (i, 0))
    return pl.pallas_call(
        functools.partial(_prologue_body, n_prompt_blocks=npb),
        grid=(t // tm,),
        in_specs=[
            pl.BlockSpec((tm, D_MODEL), lambda i: (jnp.minimum(i, npb - 1), 0)),
            pl.BlockSpec((tm, D_MODEL), lambda i: (jnp.maximum(i - npb, 0), 0)),
            pl.BlockSpec((1, 1, D_MODEL), lambda i: (_seq_of_row(i * tm, lay) * 6 + 0, 0, 0)),
            pl.BlockSpec((1, 1, D_MODEL), lambda i: (_seq_of_row(i * tm, lay) * 6 + 1, 0, 0)),
            pl.BlockSpec((1, D_MODEL), lambda i: (0, 0)),
        ],
        out_specs=[blk, blk],
        out_shape=[jax.ShapeDtypeStruct((t, D_MODEL), F32), jax.ShapeDtypeStruct((t, D_MODEL), BF16)],
        compiler_params=_params(("arbitrary",)),
    )(x_prompt, x_sample, modr, modr, norm_g.reshape(1, D_MODEL))


def _proj_in_body(h_ref, wqkv_ref, wga_ref, wgb_ref, qkv_ref, u_ref):
    j = pl.program_id(1)

    @pl.when(j < 3)
    def _():
        r = jnp.dot(h_ref[...], wqkv_ref[...], preferred_element_type=F32)
        r = r * jnp.where(j == 0, HEAD_DIM ** -0.5, 1.0).astype(F32)
        for hh in range(N_HEADS):
            qkv_ref[hh] = r[:, hh * HEAD_DIM:(hh + 1) * HEAD_DIM]

    @pl.when(j == 3)
    def _():
        h = h_ref[...]
        a = jnp.dot(h, wga_ref[...], preferred_element_type=F32)
        b = jnp.dot(h, wgb_ref[...], preferred_element_type=F32)
        u_ref[...] = a * jax.nn.sigmoid(b)


def _proj_in(h, w_in_bf, tm):
    t = h.shape[0]
    tn = ATTN_WIDTH
    assert CONV_CH == tn
    resident = dict(pipeline_mode=pl.Buffered(1))
    return pl.pallas_call(
        _proj_in_body,
        grid=(t // tm, 4),
        in_specs=[
            pl.BlockSpec((tm, D_MODEL), lambda i, j: (i, 0)),
            pl.BlockSpec((D_MODEL, tn), lambda i, j: (0, jnp.minimum(j, 2))),
            pl.BlockSpec((D_MODEL, tn), lambda i, j: (0, 3), **resident),
            pl.BlockSpec((D_MODEL, tn), lambda i, j: (0, 4), **resident),
        ],
        out_specs=[
            pl.BlockSpec((N_HEADS, tm, HEAD_DIM), lambda i, j: (jnp.minimum(j, 2), i, 0)),
            pl.BlockSpec((tm, CONV_CH), lambda i, j: (i, 0)),
        ],
        out_shape=[
            jax.ShapeDtypeStruct((3 * N_HEADS, t, HEAD_DIM), F32),
            jax.ShapeDtypeStruct((t, CONV_CH), F32),
        ],
        compiler_params=_params(("arbitrary", "arbitrary")),
    )(h, w_in_bf, w_in_bf, w_in_bf)


def _alibi_slopes():
    return jnp.exp2(-8.0 * jnp.arange(1, N_HEADS + 1, dtype=F32) / N_HEADS)


def _attn_bias():
    i = jnp.arange(ATT_TQ)[:, None]
    j = jnp.arange(ATT_TK)[None, :]
    out = []
    for _, d in DILATED_CONFIGS:
        for v in range(3):
            rel = jnp.abs(j - ATT_HALF * v - i)
            dist = (d * rel).astype(F32)
            b = -_alibi_slopes()[:, None, None] * dist[None]
            out.append(jnp.where((rel <= ATT_HALF)[None], b, MASK_VALUE))
    return jnp.stack(out, axis=1)


def _strided_rows(start, size, stride):
    return pl.ds(start, size, stride=stride) if stride > 1 else pl.ds(start, size)


def _stage_two_level(q_ref, k_ref, v_ref, tmp, qs, ks, vs, seq_len):
    n1 = seq_len // ATT_SPLIT
    n2 = n1 // ATT_SPLIT
    for a in range(ATT_SPLIT):
        def dense(ch, cr, a=a):
            src = pl.ds(a + ch * (ATT_STAGE * ATT_SPLIT), ATT_STAGE, stride=ATT_SPLIT)
            dst = pl.ds(pl.multiple_of(ch * ATT_STAGE, ATT_STAGE), ATT_STAGE)
            tmp[0, dst, :] = q_ref[0, src, :]
            tmp[1, dst, :] = k_ref[0, src, :]
            tmp[2, dst, :] = v_ref[0, src, :]
            return cr

        lax.fori_loop(0, n1 // ATT_STAGE, dense, 0, unroll=2)
        for b in range(ATT_SPLIT):
            def cast(ch, cr, a=a, b=b):
                src = pl.ds(b + ch * (ATT_STAGE * ATT_SPLIT), ATT_STAGE, stride=ATT_SPLIT)
                dst = pl.ds(pl.multiple_of((a + ATT_SPLIT * b) * n2 + ch * ATT_STAGE, ATT_STAGE), ATT_STAGE)
                qs[dst, :] = tmp[0, src, :].astype(BF16)
                ks[dst, :] = tmp[1, src, :].astype(BF16)
                vs[dst, :] = tmp[2, src, :].astype(BF16)
                return cr

            lax.fori_loop(0, n2 // ATT_STAGE, cast, 0)


def _attn_body(q_ref, k_ref, v_ref, bias_ref, *rest, seq_len):
    o_ref, lse_scr, qs, ks, vs, tmp = rest
    order = sorted(range(len(DILATED_CONFIGS)), key=lambda c: -DILATED_CONFIGS[c][1])
    for step, c in enumerate(order):
        d = DILATED_CONFIGS[c][1]
        sub_len = seq_len // d
        first, last = step == 0, step == len(order) - 1

        nstage = sub_len // ATT_STAGE
        nt = sub_len // ATT_TQ

        def stage(n, cr, d=d, sub_len=sub_len, nstage=nstage):
            r = lax.div(n, nstage)
            ch = lax.rem(n, nstage)
            dst = pl.ds(pl.multiple_of(n * ATT_STAGE, ATT_STAGE), ATT_STAGE)
            src = _strided_rows(r + ch * (ATT_STAGE * d), ATT_STAGE, d)
            qs[dst, :] = q_ref[0, src, :].astype(BF16)
            ks[dst, :] = k_ref[0, src, :].astype(BF16)
            vs[dst, :] = v_ref[0, src, :].astype(BF16)
            return cr

        if d == ATT_SPLIT * ATT_SPLIT:
            _stage_two_level(q_ref, k_ref, v_ref, tmp, qs, ks, vs, seq_len)
        else:
            lax.fori_loop(0, d * nstage, stage, 0, unroll=2)

        def tile(n, ct, c=c, d=d, sub_len=sub_len, nt=nt, first=first, last=last):
            r = lax.div(n, nt)
            l0 = lax.rem(n, nt) * ATT_TQ
            k0 = jnp.clip(l0 - ATT_HALF, 0, sub_len - ATT_TK)
            var = lax.div(l0 - k0, ATT_HALF)
            base = r * sub_len
            q = qs[pl.ds(pl.multiple_of(base + l0, ATT_TQ), ATT_TQ), :]
            k = ks[pl.ds(pl.multiple_of(base + k0, ATT_HALF), ATT_TK), :]
            v = vs[pl.ds(pl.multiple_of(base + k0, ATT_HALF), ATT_TK), :]
            s = lax.dot_general(q, k, (((1,), (1,)), ((), ())), preferred_element_type=F32)
            s = s + bias_ref[0, 3 * c + var]
            m = jnp.max(s, axis=-1, keepdims=True)
            p = jnp.exp(s - m)
            l = jnp.sum(p, axis=-1, keepdims=True)
            o = jnp.dot(p.astype(BF16), v, preferred_element_type=F32)
            mb = jnp.broadcast_to(m, (ATT_TQ, LANES))
            lb = jnp.broadcast_to(l, (ATT_TQ, LANES))
            rows = _strided_rows(r + l0 * d, ATT_TQ, d)
            if first:
                o_ref[rows, :] = o / lb
                lse_scr[rows, :] = mb + jnp.log(lb)
            else:
                la = lse_scr[rows, :]
                mx = jnp.maximum(la, mb)
                wa = jnp.exp(la - mx)
                wb = jnp.exp(mb - mx)
                den = wa + wb * lb
                o_ref[rows, :] = (wa * o_ref[rows, :] + wb * o) / den
                if not last:
                    lse_scr[rows, :] = mx + jnp.log(den)
            return ct

        lax.fori_loop(0, d * nt, tile, 0, unroll=ATT_UNROLL)


def _attention(qkv, bias, lay, *, group):
    batch, seq, dec_batch, dec_seq = lay
    if group == 0:
        nb, s_len, blk0 = batch, seq, 0
    else:
        assert (batch * seq) % dec_seq == 0
        nb, s_len, blk0 = dec_batch, dec_seq, batch * seq // dec_seq
    assert s_len % (ATT_TK * DILATED_CONFIGS[-1][1]) == 0

    def hm(part):
        return lambda b, h: (part * N_HEADS + h, blk0 + b, 0)

    return pl.pallas_call(
        functools.partial(_attn_body, seq_len=s_len),
        grid=(nb, N_HEADS),
        in_specs=[
            pl.BlockSpec((1, s_len, HEAD_DIM), hm(0)),
            pl.BlockSpec((1, s_len, HEAD_DIM), hm(1)),
            pl.BlockSpec((1, s_len, HEAD_DIM), hm(2)),
            pl.BlockSpec((1, bias.shape[1], ATT_TQ, ATT_TK), lambda b, h: (h, 0, 0, 0)),
        ],
        out_specs=pl.BlockSpec((s_len, HEAD_DIM), lambda b, h: (b, h)),
        out_shape=jax.ShapeDtypeStruct((nb * s_len, ATTN_WIDTH), F32),
        scratch_shapes=([pltpu.VMEM((s_len, LANES), F32)] + [pltpu.VMEM((s_len, HEAD_DIM), BF16)] * 3
                        + [pltpu.VMEM((3, s_len // ATT_SPLIT, HEAD_DIM), F32)]),
        compiler_params=_params(("arbitrary", "arbitrary")),
    )(qkv, qkv, qkv, bias)


def _conv_body(prev_ref, cur_ref, next_ref, w_ref, cb_ref, lg_ref, lb_ref, og_ref, o_ref,
               pad_scr, cv_scr, *, lay, ts):
    batch, seq, dec_batch, dec_seq = lay
    n_prompt = batch * seq
    row0 = pl.program_id(0) * ts
    row1 = row0 + ts
    pos0 = jnp.where(row0 < n_prompt, row0 % seq, (row0 - n_prompt) % dec_seq)
    pos1 = jnp.where(row1 <= n_prompt, row1 % seq, (row1 - n_prompt) % dec_seq)
    nch = CONV_CH // LANES
    lanes = [slice(cc * LANES, (cc + 1) * LANES) for cc in range(nch)]
    for cc in range(nch):
        pad_scr[cc, 0:CONV_HALO, :] = jnp.where(pos0 != 0, prev_ref[:, lanes[cc]], 0.0)
        pad_scr[cc, CONV_HALO:CONV_HALO + ts, :] = cur_ref[:, lanes[cc]]
        pad_scr[cc, CONV_HALO + ts:, :] = jnp.where(pos1 != 0, next_ref[:, lanes[cc]], 0.0)

    rc = 128
    base = CONV_HALO - CONV_WIDTH // 2

    def lane_chunk(cc, carry):
        for r in range(ts // rc):
            for par in range(2):
                acc = jnp.zeros((rc // 2, LANES), F32)
                for j in range(CONV_WIDTH):
                    acc = acc + (pad_scr[cc, pl.ds(base + r * rc + par + j, rc // 2, stride=2), :]
                                 * w_ref[cc, pl.ds(j, 1), :])
                cv_scr[cc, pl.ds(r * rc + par, rc // 2, stride=2), :] = acc
        return carry

    lax.fori_loop(0, nch, lane_chunk, 0)

    rn = 32
    inv_c = 1.0 / CONV_CH

    def row_chunk(r, carry):
        r0 = pl.multiple_of(r * rn, rn)
        u = [cv_scr[cc, pl.ds(r0, rn), :] + cb_ref[:, lanes[cc]] for cc in range(nch)]
        mu = jnp.sum(sum(u), axis=-1, keepdims=True) * inv_c
        u = [a - mu for a in u]
        var = jnp.sum(sum(a * a for a in u), axis=-1, keepdims=True) * inv_c
        rstd = lax.rsqrt(var + NORM_EPS)
        y = [a * rstd * lg_ref[:, lanes[cc]] + lb_ref[:, lanes[cc]] for cc, a in enumerate(u)]
        y = [a * jax.nn.sigmoid(a) for a in y]
        ms = jnp.sum(sum(a * a for a in y), axis=-1, keepdims=True) * inv_c
        rr = lax.rsqrt(ms + NORM_EPS)
        for cc, a in enumerate(y):
            o_ref[pl.ds(r0, rn), lanes[cc]] = (a * rr * og_ref[:, lanes[cc]]).astype(o_ref.dtype)
        return carry

    lax.fori_loop(0, ts // rn, row_chunk, 0, unroll=4)


def _conv(u, conv_w, conv_b, ln_g, ln_b, out_g, lay, ts):
    t = u.shape[0]
    hb = ts // CONV_HALO
    nhb = t // CONV_HALO
    nch = CONV_CH // LANES
    w_slabs = conv_w.reshape(CONV_WIDTH, nch, LANES).transpose(1, 0, 2)
    row = lambda a: a.reshape(1, CONV_CH)
    return pl.pallas_call(
        functools.partial(_conv_body, lay=lay, ts=ts),
        grid=(t // ts,),
        in_specs=[
            pl.BlockSpec((CONV_HALO, CONV_CH), lambda i: (jnp.maximum(i * hb - 1, 0), 0)),
            pl.BlockSpec((ts, CONV_CH), lambda i: (i, 0)),
            pl.BlockSpec((CONV_HALO, CONV_CH), lambda i: (jnp.minimum((i + 1) * hb, nhb - 1), 0)),
            pl.BlockSpec((nch, CONV_WIDTH, LANES), lambda i: (0, 0, 0)),
            pl.BlockSpec((1, CONV_CH), lambda i: (0, 0)),
            pl.BlockSpec((1, CONV_CH), lambda i: (0, 0)),
            pl.BlockSpec((1, CONV_CH), lambda i: (0, 0)),
            pl.BlockSpec((1, CONV_CH), lambda i: (0, 0)),
        ],
        out_specs=pl.BlockSpec((ts, CONV_CH), lambda i: (i, 0)),
        out_shape=jax.ShapeDtypeStruct((t, CONV_CH), BF16),
        scratch_shapes=[pltpu.VMEM((nch, ts + 2 * CONV_HALO, LANES), F32),
                        pltpu.VMEM((nch, ts, LANES), F32)],
        compiler_params=_params(("arbitrary",)),
    )(u, u, u, w_slabs, row(conv_b), row(ln_g), row(ln_b), row(out_g))


def _proj_out_body(attn_p_ref, attn_s_ref, conv_ref, x_ref, ag_ref, wout_ref, gate_ref, g2_ref, shift_ref,
                   scale_ref, wr_ref, br_ref, xmid_ref, h_ref, ri_ref, rf_ref, *, n_prompt_blocks):
    for s in range(attn_p_ref.shape[0] // PROJ_OUT_SUB):
        _proj_out_rows(slice(s * PROJ_OUT_SUB, (s + 1) * PROJ_OUT_SUB), attn_p_ref, attn_s_ref, conv_ref,
                       x_ref, ag_ref, wout_ref, gate_ref, g2_ref, shift_ref, scale_ref, wr_ref, br_ref,
                       xmid_ref, h_ref, ri_ref, rf_ref, n_prompt_blocks)


def _proj_out_rows(rs, attn_p_ref, attn_s_ref, conv_ref, x_ref, ag_ref, wout_ref, gate_ref, g2_ref, shift_ref,
                   scale_ref, wr_ref, br_ref, xmid_ref, h_ref, ri_ref, rf_ref, n_prompt_blocks):
    a = jnp.where(pl.program_id(0) < n_prompt_blocks, attn_p_ref[rs, :], attn_s_ref[rs, :])
    ms = jnp.mean(a * a, axis=-1, keepdims=True)
    an = (a * lax.rsqrt(ms + NORM_EPS) * ag_ref[...]).astype(BF16)
    o = jnp.dot(an, wout_ref[0:ATTN_WIDTH, :], preferred_element_type=F32)
    o = o + jnp.dot(conv_ref[rs, :], wout_ref[ATTN_WIDTH:, :], preferred_element_type=F32)
    x = x_ref[rs, :] + gate_ref[0] * o
    xmid_ref[rs, :] = x
    hb = _norm_mod(x, g2_ref, scale_ref, shift_ref)
    h_ref[rs, :] = _pack_rows(hb)

    lg = jnp.dot(hb, wr_ref[...], preferred_element_type=F32) + br_ref[...]
    lane = lax.broadcasted_iota(jnp.int32, lg.shape, 1)
    lane_f = lane.astype(F32)
    neg = jnp.float32(-jnp.inf)
    big = jnp.float32(ROUTE_LANES)
    gmask = lane < N_GROUPS
    lgm = jnp.where(gmask, lg, neg)
    gmax = jnp.max(lgm, axis=-1, keepdims=True)
    gidx = jnp.min(jnp.where(lgm == gmax, lane_f, big), axis=-1, keepdims=True).astype(jnp.int32)
    gsum = jnp.sum(jnp.where(gmask, jnp.exp(lg - gmax), 0.0), axis=-1, keepdims=True)
    pg = 1.0 / gsum
    lo = N_GROUPS + gidx * EXPERTS_PER_GROUP
    emask = (lane >= lo) & (lane < lo + EXPERTS_PER_GROUP)
    le1 = jnp.where(emask, lg, neg)
    e1 = jnp.max(le1, axis=-1, keepdims=True)
    i1 = jnp.min(jnp.where(le1 == e1, lane_f, big), axis=-1, keepdims=True)
    le2 = jnp.where(lane_f == i1, neg, le1)
    e2 = jnp.max(le2, axis=-1, keepdims=True)
    i2 = jnp.min(jnp.where(le2 == e2, lane_f, big), axis=-1, keepdims=True)
    r = jnp.exp(e2 - e1)
    w1 = pg / (1.0 + r)
    w2 = pg * r / (1.0 + r)
    id1 = (i1 - N_GROUPS).astype(jnp.int32)
    id2 = (i2 - N_GROUPS).astype(jnp.int32)
    ri_ref[rs, :] = jnp.where(lane == 0, id1, jnp.where(lane == 1, id2, 0))
    rf_ref[rs, :] = jnp.where(lane == 0, w1, jnp.where(lane == 1, w2, 0.0))


def _proj_out(attn_p, attn_s, convn, x, modr, attn_g, w_out_bf, norm2_g, wr, br, lay, tm):
    t = x.shape[0]
    npb = attn_p.shape[0] // tm

    def seq(i):
        return _seq_of_row(i * tm, lay)

    row = lambda a, n: a.reshape(1, n)
    return pl.pallas_call(
        functools.partial(_proj_out_body, n_prompt_blocks=npb),
        grid=(t // tm,),
        in_specs=[
            pl.BlockSpec((tm, ATTN_WIDTH), lambda i: (jnp.minimum(i, npb - 1), 0)),
            pl.BlockSpec((tm, ATTN_WIDTH), lambda i: (jnp.maximum(i - npb, 0), 0)),
            pl.BlockSpec((tm, CONV_CH), lambda i: (i, 0)),
            pl.BlockSpec((tm, D_MODEL), lambda i: (i, 0)),
            pl.BlockSpec((1, ATTN_WIDTH), lambda i: (0, 0)),
            pl.BlockSpec((D_MODEL, D_MODEL), lambda i: (0, 0), pipeline_mode=pl.Buffered(1)),
            pl.BlockSpec((1, 1, D_MODEL), lambda i: (seq(i) * 6 + 2, 0, 0)),
            pl.BlockSpec((1, D_MODEL), lambda i: (0, 0)),
            pl.BlockSpec((1, 1, D_MODEL), lambda i: (seq(i) * 6 + 3, 0, 0)),
            pl.BlockSpec((1, 1, D_MODEL), lambda i: (seq(i) * 6 + 4, 0, 0)),
            pl.BlockSpec((D_MODEL, ROUTE_LANES), lambda i: (0, 0)),
            pl.BlockSpec((1, ROUTE_LANES), lambda i: (0, 0)),
        ],
        out_specs=[
            pl.BlockSpec((tm, D_MODEL), lambda i: (i, 0)),
            pl.BlockSpec((tm, D_MODEL // 2), lambda i: (i, 0)),
            pl.BlockSpec((tm, ROUTE_LANES), lambda i: (i, 0)),
            pl.BlockSpec((tm, ROUTE_LANES), lambda i: (i, 0)),
        ],
        out_shape=[
            jax.ShapeDtypeStruct((t, D_MODEL), F32),
            jax.ShapeDtypeStruct((t, D_MODEL // 2), jnp.uint32),
            jax.ShapeDtypeStruct((t, ROUTE_LANES), jnp.int32),
            jax.ShapeDtypeStruct((t, ROUTE_LANES), F32),
        ],
        compiler_params=_params(("arbitrary",)),
    )(attn_p, attn_s, convn, x, row(attn_g, ATTN_WIDTH), w_out_bf, modr, row(norm2_g, D_MODEL), modr, modr,
      wr, br)


def _dispatch(experts, tm):
    t = experts.shape[0]
    a = t * TOP_K
    nb = -(-(a + N_EXPERTS * (tm - 1)) // tm)
    flat_e = experts.reshape(-1)
    order = jnp.argsort(flat_e).astype(jnp.int32)
    rank = jnp.argsort(order).astype(jnp.int32)
    counts = jnp.sum(flat_e[:, None] == jnp.arange(N_EXPERTS)[None, :], axis=0).astype(jnp.int32)
    starts = jnp.cumsum(counts) - counts
    padded = ((counts + tm - 1) // tm) * tm
    padded_ends = jnp.cumsum(padded)
    padded_starts = padded_ends - padded
    shift = (padded_starts - starts).astype(jnp.int32)
    pos = (rank + shift[flat_e]).reshape(t, TOP_K)
    block_expert = jnp.minimum(
        jnp.sum(padded_ends[None, :] <= (jnp.arange(nb) * tm)[:, None], axis=1), N_EXPERTS - 1).astype(jnp.int32)
    row = jnp.arange(nb * tm, dtype=jnp.int32)
    row_e = jnp.repeat(block_expert, tm)
    sorted_pos = row - shift[row_e]
    valid = sorted_pos < (starts + counts)[row_e]
    src = order[jnp.clip(sorted_pos, 0, a - 1)]
    row_tok = jnp.where(valid, src // TOP_K, row % t).astype(jnp.int32)
    n_valid = (padded_ends[-1] // tm).astype(jnp.int32).reshape(1)
    nonempty = counts > 0
    slot = ((jnp.cumsum(nonempty) - nonempty) % 2).astype(jnp.int32)
    ids = jnp.where(nonempty, jnp.arange(N_EXPERTS), N_EXPERTS)
    nxt = lax.cummin(jnp.concatenate([ids[1:], jnp.array([N_EXPERTS])]), reverse=True)
    nxt = jnp.where(nxt >= N_EXPERTS, -1, nxt).astype(jnp.int32)
    run_info = jnp.concatenate([slot, nxt, (padded_starts // tm).astype(jnp.int32)])
    return row_tok, pos, block_expert, n_valid, run_info


def _expert_weights(be_ref, ri_ref, hbm_refs, bufs, sem, layer, tn):
    j, i = pl.program_id(0), pl.program_id(1)
    e = be_ref[i]
    slot, nxt, first_blk = ri_ref[e], ri_ref[N_EXPERTS + e], ri_ref[2 * N_EXPERTS + e]

    def copies(expert, s):
        return [pltpu.make_async_copy(h.at[layer, expert, :, pl.ds(pl.multiple_of(j * tn, tn), tn)], b.at[s],
                                      sem.at[n, s])
                for n, (h, b) in enumerate(zip(hbm_refs, bufs))]

    @pl.when(i == first_blk)
    def _():
        @pl.when(i == 0)
        def _():
            for c in copies(e, slot):
                c.start()

        for c in copies(e, slot):
            c.wait()

        @pl.when(nxt >= 0)
        def _():
            for c in copies(nxt, 1 - slot):
                c.start()

    return slot


def _gate_up_body(be_ref, nv_ref, ri_ref, xs_ref, wg_hbm, wu_hbm, a_ref, wg_buf, wu_buf, sem, *, layer, tn):
    i = pl.program_id(1)
    half = D_MODEL // 2

    @pl.when(i < nv_ref[0])
    def _():
        slot = _expert_weights(be_ref, ri_ref, (wg_hbm, wu_hbm), (wg_buf, wu_buf), sem, layer, tn)
        lo, hi = (v.astype(BF16) for v in _unpack_rows(xs_ref[...]))
        g = (jnp.dot(lo, wg_buf[slot, :half, :].astype(BF16), preferred_element_type=F32)
             + jnp.dot(hi, wg_buf[slot, half:, :].astype(BF16), preferred_element_type=F32))
        u = (jnp.dot(lo, wu_buf[slot, :half, :].astype(BF16), preferred_element_type=F32)
             + jnp.dot(hi, wu_buf[slot, half:, :].astype(BF16), preferred_element_type=F32))
        a_ref[...] = (g * jax.nn.sigmoid(g) * u).astype(a_ref.dtype)

    @pl.when(i >= nv_ref[0])
    def _():
        a_ref[...] = jnp.zeros_like(a_ref)


def _gate_up(xs, e_gate, e_up, block_expert, n_valid, run_info, layer, tm):
    p = xs.shape[0]
    tn = D_EXPERT
    return pl.pallas_call(
        functools.partial(_gate_up_body, layer=layer, tn=tn),
        grid_spec=pltpu.PrefetchScalarGridSpec(
            num_scalar_prefetch=3,
            grid=(D_EXPERT // tn, p // tm),
            in_specs=[
                pl.BlockSpec((tm, D_MODEL // 2), lambda j, i, be, nv, ri: (jnp.minimum(i, nv[0] - 1), 0)),
                pl.BlockSpec(memory_space=pl.ANY),
                pl.BlockSpec(memory_space=pl.ANY),
            ],
            out_specs=pl.BlockSpec((tm, tn), lambda j, i, be, nv, ri: (i, j)),
            scratch_shapes=[pltpu.VMEM((2, D_MODEL, tn), F32), pltpu.VMEM((2, D_MODEL, tn), F32),
                            pltpu.SemaphoreType.DMA((2, 2))],
        ),
        out_shape=jax.ShapeDtypeStruct((p, D_EXPERT), BF16),
        compiler_params=_params(("arbitrary", "arbitrary")),
    )(block_expert, n_valid, run_info, xs, e_gate, e_up)


def _down_body(be_ref, nv_ref, ri_ref, a_ref, wd_hbm, y_ref, wd_buf, sem, *, layer, tn):
    i = pl.program_id(1)

    @pl.when(i < nv_ref[0])
    def _():
        slot = _expert_weights(be_ref, ri_ref, (wd_hbm,), (wd_buf,), sem, layer, tn)
        y_ref[...] = _pack_rows(jnp.dot(a_ref[...], wd_buf[slot].astype(BF16), preferred_element_type=F32))

    @pl.when(i >= nv_ref[0])
    def _():
        y_ref[...] = jnp.zeros_like(y_ref)


def _down(a, e_down, block_expert, n_valid, run_info, layer, tm):
    p = a.shape[0]
    tn = D_MODEL
    return pl.pallas_call(
        functools.partial(_down_body, layer=layer, tn=tn),
        grid_spec=pltpu.PrefetchScalarGridSpec(
            num_scalar_prefetch=3,
            grid=(D_MODEL // tn, p // tm),
            in_specs=[
                pl.BlockSpec((tm, D_EXPERT), lambda j, i, be, nv, ri: (jnp.minimum(i, nv[0] - 1), 0)),
                pl.BlockSpec(memory_space=pl.ANY),
            ],
            out_specs=pl.BlockSpec((tm, tn // 2), lambda j, i, be, nv, ri: (i, j)),
            scratch_shapes=[pltpu.VMEM((2, D_EXPERT, tn), F32), pltpu.SemaphoreType.DMA((1, 2))],
        ),
        out_shape=jax.ShapeDtypeStruct((p, D_MODEL // 2), jnp.uint32),
        compiler_params=_params(("arbitrary", "arbitrary")),
    )(block_expert, n_valid, run_info, a, e_down)


def _moe_residual(x_ref, y0_ref, y1_ref, rf_ref, gate_ref):
    rf = rf_ref[...]
    y0 = jnp.concatenate(_unpack_rows(y0_ref[...]), axis=1)
    y1 = jnp.concatenate(_unpack_rows(y1_ref[...]), axis=1)
    return x_ref[...] + gate_ref[0] * (rf[:, 0:1] * y0 + rf[:, 1:2] * y1)


def _combine_body(x_ref, y0_ref, y1_ref, rf_ref, gate_ref, shift_ref, scale_ref, g_ref, o_ref, h_ref):
    x = _moe_residual(x_ref, y0_ref, y1_ref, rf_ref, gate_ref)
    o_ref[...] = x
    h_ref[...] = _norm_mod(x, g_ref, scale_ref, shift_ref)


def _combine(x, y0, y1, rf, modr, modr_next, norm_g_next, lay, tm):
    t = x.shape[0]
    blk = pl.BlockSpec((tm, D_MODEL), lambda i: (i, 0))
    yblk = pl.BlockSpec((tm, D_MODEL // 2), lambda i: (i, 0))

    def mod_row(k):
        return pl.BlockSpec((1, 1, D_MODEL), lambda i: (_seq_of_row(i * tm, lay) * 6 + k, 0, 0))

    return pl.pallas_call(
        _combine_body,
        grid=(t // tm,),
        in_specs=[blk, yblk, yblk, pl.BlockSpec((tm, ROUTE_LANES), lambda i: (i, 0)),
                  mod_row(5), mod_row(0), mod_row(1), pl.BlockSpec((1, D_MODEL), lambda i: (0, 0))],
        out_specs=[blk, blk],
        out_shape=[jax.ShapeDtypeStruct((t, D_MODEL), F32), jax.ShapeDtypeStruct((t, D_MODEL), BF16)],
        compiler_params=_params(("arbitrary",)),
    )(x, y0, y1, rf, modr, modr_next, modr_next, norm_g_next.reshape(1, D_MODEL))


def _final_body(x_ref, y0_ref, y1_ref, rf_ref, gate_ref, g_ref, op_ref, os_ref, *, n_prompt_blocks):
    i = pl.program_id(0)
    x = _moe_residual(x_ref, y0_ref, y1_ref, rf_ref, gate_ref)
    ms = jnp.mean(x * x, axis=-1, keepdims=True)
    y = x * lax.rsqrt(ms + NORM_EPS) * g_ref[...]

    @pl.when(i < n_prompt_blocks)
    def _():
        op_ref[...] = y

    @pl.when(i >= n_prompt_blocks)
    def _():
        os_ref[...] = y


def _final(x, y0, y1, rf, modr, final_g, lay, tm):
    batch, seq, dec_batch, dec_seq = lay
    t = x.shape[0]
    n_prompt = batch * seq
    npb = n_prompt // tm
    blk = pl.BlockSpec((tm, D_MODEL), lambda i: (i, 0))
    yblk = pl.BlockSpec((tm, D_MODEL // 2), lambda i: (i, 0))
    return pl.pallas_call(
        functools.partial(_final_body, n_prompt_blocks=npb),
        grid=(t // tm,),
        in_specs=[blk, yblk, yblk, pl.BlockSpec((tm, ROUTE_LANES), lambda i: (i, 0)),
                  pl.BlockSpec((1, 1, D_MODEL), lambda i: (_seq_of_row(i * tm, lay) * 6 + 5, 0, 0)),
                  pl.BlockSpec((1, D_MODEL), lambda i: (0, 0))],
        out_specs=[
            pl.BlockSpec((tm, D_MODEL), lambda i: (jnp.minimum(i, npb - 1), 0)),
            pl.BlockSpec((tm, D_MODEL), lambda i: (jnp.maximum(i - npb, 0), 0)),
        ],
        out_shape=[
            jax.ShapeDtypeStruct((n_prompt, D_MODEL), F32),
            jax.ShapeDtypeStruct((t - n_prompt, D_MODEL), F32),
        ],
        compiler_params=_params(("arbitrary",)),
    )(x, y0, y1, rf, modr, final_g.reshape(1, D_MODEL))


def _trunk(x_prompt, x_sample, c_all, lay, ada_w, ada_b, norm1_g, w_in, conv_w, conv_b, conv_ln_g, conv_ln_b,
           attn_out_g, conv_out_g, w_out, norm2_g, rg_w, rg_b, re_w, re_b, e_gate, e_up, e_down,
           final_g, *, tm_in=1024, tm_out=512, ts_conv=512, tm_moe=512, tm_res=512):
    batch, seq, dec_batch, dec_seq = lay
    ns = c_all.shape[0]
    ns8 = -(-ns // 8) * 8
    c_pad = jnp.zeros((ns8, D_MODEL), F32).at[:ns].set(c_all)
    mod = _ada_mod(c_pad, ada_w, ada_b)
    modrs = [mod[l, :ns].reshape(ns * 6, 1, D_MODEL) for l in range(DEPTH)]
    bias = _attn_bias()
    n_route = N_GROUPS + N_EXPERTS
    y_prompt = y_sample = None
    x, h = _prologue(x_prompt, x_sample, modrs[0], norm1_g[0], lay, tm_res)
    for l in range(DEPTH):
        modr = modrs[l]
        qkv, u = _proj_in(h, w_in[l].astype(BF16), tm_in)
        attn_p = _attention(qkv, bias, lay, group=0)
        attn_s = _attention(qkv, bias, lay, group=1)
        convn = _conv(u, conv_w[l], conv_b[l], conv_ln_g[l], conv_ln_b[l], conv_out_g[l], lay, ts_conv)
        wr = jnp.zeros((D_MODEL, ROUTE_LANES), F32).at[:, :n_route].set(
            jnp.concatenate([rg_w[l], re_w[l]], axis=1)).astype(BF16)
        br = jnp.zeros((1, ROUTE_LANES), F32).at[0, :n_route].set(jnp.concatenate([rg_b[l], re_b[l]]))
        x_mid, h2, ri, rf = _proj_out(attn_p, attn_s, convn, x, modr, attn_out_g[l], w_out[l].astype(BF16),
                                      norm2_g[l], wr, br, lay, tm_out)
        row_tok, pos, block_expert, n_valid, run_info = _dispatch(ri[:, :TOP_K], tm_moe)
        xs = h2.at[row_tok].get(mode='promise_in_bounds')
        act = _gate_up(xs, e_gate, e_up, block_expert, n_valid, run_info, l, tm_moe)
        ys = _down(act, e_down, block_expert, n_valid, run_info, l, tm_moe)
        y0 = ys.at[pos[:, 0]].get(mode='promise_in_bounds')
        y1 = ys.at[pos[:, 1]].get(mode='promise_in_bounds')
        if l < DEPTH - 1:
            x, h = _combine(x_mid, y0, y1, rf, modr, modrs[l + 1], norm1_g[l + 1], lay, tm_res)
        else:
            y_prompt, y_sample = _final(x_mid, y0, y1, rf, modr, final_g, lay, tm_res)
    return (y_prompt.reshape(batch, seq, D_MODEL), y_sample.reshape(dec_batch, dec_seq, D_MODEL))


def kernel(x_prompt, x_sample, c_prompt, c_sample, ada_w, ada_b, norm1_g, w_in, conv_w, conv_b, conv_ln_g, conv_ln_b, attn_out_g, conv_out_g, w_out, norm2_g, rg_w, rg_b, re_w, re_b, e_gate, e_up, e_down, final_g):
    batch, seq, _ = x_prompt.shape
    dec_batch, dec_seq, _ = x_sample.shape
    lay = (batch, seq, dec_batch, dec_seq)
    c_all = jnp.concatenate([c_prompt, c_sample], axis=0)
    return _trunk(x_prompt.reshape(batch * seq, D_MODEL), x_sample.reshape(dec_batch * dec_seq, D_MODEL),
                  c_all, lay, ada_w, ada_b, norm1_g, w_in, conv_w, conv_b, conv_ln_g, conv_ln_b,
                  attn_out_g, conv_out_g, w_out, norm2_g, rg_w, rg_b, re_w, re_b,
                  e_gate, e_up, e_down, final_g)
```

```python
import functools
import math

import jax
import jax.numpy as jnp
from jax import lax
from jax.experimental import pallas as pl
from jax.experimental.pallas import tpu as pltpu

D_MODEL = 2048
DEPTH = 4
N_HEADS = 8
HEAD_DIM = 128
ATTN_WIDTH = N_HEADS * HEAD_DIM
CONV_CH = D_MODEL - ATTN_WIDTH
CONV_WIDTH = 31
DILATED_CONFIGS = ((128, 1), (512, 4), (2048, 16))
N_GROUPS = 4
EXPERTS_PER_GROUP = 8
N_EXPERTS = N_GROUPS * EXPERTS_PER_GROUP
TOP_K = 2
D_EXPERT = 1024
NORM_EPS = 1e-6
MASK_VALUE = -1e30
LOG2_E = math.log2(math.e)

LANES = 128
ATT_TQ = 128
ATT_TK = 256
ATT_HALF = 64
ATT_STAGE = 256
ATT_UNROLL = 32
ATT_SPLIT = 4
CONV_HALO = 16
ROUTE_LANES = 128
PROJ_OUT_SUB = 256
VMEM_LIMIT = 56 * 1024 * 1024

F32 = jnp.float32
BF16 = jnp.bfloat16


def _params(sem, vmem=VMEM_LIMIT):
    return pltpu.CompilerParams(dimension_semantics=sem, vmem_limit_bytes=vmem)


def _seq_of_row(row, lay):
    batch, seq, _, dec_seq = lay
    n_prompt = batch * seq
    return jnp.where(row < n_prompt, row // seq, batch + (row - n_prompt) // dec_seq)


def _ada_body(c_ref, w_ref, b_ref, o_ref):
    c = c_ref[...]
    sc = c * jax.nn.sigmoid(c)
    o_ref[0] = jnp.dot(sc, w_ref[0], preferred_element_type=F32) + b_ref[0]


def _ada_mod(c_pad, ada_w, ada_b):
    ns8 = c_pad.shape[0]
    tn = 1024
    nj = 6 * D_MODEL // tn
    return pl.pallas_call(
        _ada_body,
        grid=(DEPTH, nj),
        in_specs=[
            pl.BlockSpec((ns8, D_MODEL), lambda l, j: (0, 0)),
            pl.BlockSpec((1, D_MODEL, tn), lambda l, j: (l, 0, j)),
            pl.BlockSpec((1, 1, tn), lambda l, j: (l, 0, j)),
        ],
        out_specs=pl.BlockSpec((1, ns8, tn), lambda l, j: (l, 0, j)),
        out_shape=jax.ShapeDtypeStruct((DEPTH, ns8, 6 * D_MODEL), F32),
        compiler_params=_params(("arbitrary", "arbitrary")),
    )(c_pad, ada_w, ada_b.reshape(DEPTH, 1, 6 * D_MODEL))


def _pack_rows(v):
    bits = pltpu.bitcast(v.astype(BF16).astype(F32), jnp.uint32)
    half = v.shape[1] // 2
    return (bits[:, :half] >> 16) | (bits[:, half:] & jnp.uint32(0xFFFF0000))


def _unpack_rows(words):
    return (pltpu.bitcast(words << 16, F32), pltpu.bitcast(words & jnp.uint32(0xFFFF0000), F32))


def _norm_mod(x, g_ref, scale_ref, shift_ref):
    ms = jnp.mean(x * x, axis=-1, keepdims=True)
    y = x * lax.rsqrt(ms + NORM_EPS) * g_ref[...]
    return (y * (1.0 + scale_ref[0]) + shift_ref[0]).astype(BF16)


def _prologue_body(xp_ref, xs_ref, shift_ref, scale_ref, g_ref, x_ref, h_ref, *, n_prompt_blocks):
    x = jnp.where(pl.program_id(0) < n_prompt_blocks, xp_ref[...], xs_ref[...])
    x_ref[...] = x
    h_ref[...] = _norm_mod(x, g_ref, scale_ref, shift_ref)


def _prologue(x_prompt, x_sample, modr, norm_g, lay, tm):
    n_prompt, n_sample = x_prompt.shape[0], x_sample.shape[0]
    t = n_prompt + n_sample
    npb = n_prompt // tm
    blk = pl.BlockSpec((tm, D_MODEL), lambda i: (i, 0))
    return pl.pallas_call(
        functools.partial(_prologue_body, n_prompt_blocks=npb),
        grid=(t // tm,),
        in_specs=[
            pl.BlockSpec((tm, D_MODEL), lambda i: (jnp.minimum(i, npb - 1), 0)),
            pl.BlockSpec((tm, D_MODEL), lambda i: (jnp.maximum(i - npb, 0), 0)),
            pl.BlockSpec((1, 1, D_MODEL), lambda i: (_seq_of_row(i * tm, lay) * 6 + 0, 0, 0)),
            pl.BlockSpec((1, 1, D_MODEL), lambda i: (_seq_of_row(i * tm, lay) * 6 + 1, 0, 0)),
            pl.BlockSpec((1, D_MODEL), lambda i: (0, 0)),
        ],
        out_specs=[blk, blk],
        out_shape=[jax.ShapeDtypeStruct((t, D_MODEL), F32), jax.ShapeDtypeStruct((t, D_MODEL), BF16)],
        compiler_params=_params(("arbitrary",)),
    )(x_prompt, x_sample, modr, modr, norm_g.reshape(1, D_MODEL))


def _proj_in_body(h_ref, wqkv_ref, wga_ref, wgb_ref, qkv_ref, u_ref):
    j = pl.program_id(1)

    @pl.when(j < 3)
    def _():
        r = jnp.dot(h_ref[...], wqkv_ref[...], preferred_element_type=F32)
        r = r * jnp.where(j == 0, HEAD_DIM ** -0.5 * LOG2_E, 1.0).astype(F32)
        for hh in range(N_HEADS):
            qkv_ref[hh] = r[:, hh * HEAD_DIM:(hh + 1) * HEAD_DIM]

    @pl.when(j == 3)
    def _():
        h = h_ref[...]
        a = jnp.dot(h, wga_ref[...], preferred_element_type=F32)
        b = jnp.dot(h, wgb_ref[...], preferred_element_type=F32)
        u_ref[...] = a * jax.nn.sigmoid(b)


def _proj_in(h, w_in_bf, tm):
    t = h.shape[0]
    tn = ATTN_WIDTH
    assert CONV_CH == tn
    resident = dict(pipeline_mode=pl.Buffered(1))
    return pl.pallas_call(
        _proj_in_body,
        grid=(t // tm, 4),
        in_specs=[
            pl.BlockSpec((tm, D_MODEL), lambda i, j: (i, 0)),
            pl.BlockSpec((D_MODEL, tn), lambda i, j: (0, jnp.minimum(j, 2))),
            pl.BlockSpec((D_MODEL, tn), lambda i, j: (0, 3), **resident),
            pl.BlockSpec((D_MODEL, tn), lambda i, j: (0, 4), **resident),
        ],
        out_specs=[
            pl.BlockSpec((N_HEADS, tm, HEAD_DIM), lambda i, j: (jnp.minimum(j, 2), i, 0)),
            pl.BlockSpec((tm, CONV_CH), lambda i, j: (i, 0)),
        ],
        out_shape=[
            jax.ShapeDtypeStruct((3 * N_HEADS, t, HEAD_DIM), F32),
            jax.ShapeDtypeStruct((t, CONV_CH), F32),
        ],
        compiler_params=_params(("arbitrary", "arbitrary")),
    )(h, w_in_bf, w_in_bf, w_in_bf)


def _alibi_slopes():
    return jnp.exp2(-8.0 * jnp.arange(1, N_HEADS + 1, dtype=F32) / N_HEADS)


def _attn_bias():
    i = jnp.arange(ATT_TQ)[:, None]
    j = jnp.arange(ATT_TK)[None, :]
    out = []
    for _, d in DILATED_CONFIGS:
        for v in range(3):
            rel = jnp.abs(j - ATT_HALF * v - i)
            dist = (d * rel).astype(F32)
            b = -_alibi_slopes()[:, None, None] * dist[None]
            out.append(jnp.where((rel <= ATT_HALF)[None], b * LOG2_E, MASK_VALUE))
    return jnp.stack(out, axis=1)


def _strided_rows(start, size, stride):
    return pl.ds(start, size, stride=stride) if stride > 1 else pl.ds(start, size)


def _stage_two_level(q_ref, k_ref, v_ref, tmp, qs, ks, vs, seq_len):
    n1 = seq_len // ATT_SPLIT
    n2 = n1 // ATT_SPLIT
    for a in range(ATT_SPLIT):
        def dense(ch, cr, a=a):
            src = pl.ds(a + ch * (ATT_STAGE * ATT_SPLIT), ATT_STAGE, stride=ATT_SPLIT)
            dst = pl.ds(pl.multiple_of(ch * ATT_STAGE, ATT_STAGE), ATT_STAGE)
            tmp[0, dst, :] = q_ref[0, src, :]
            tmp[1, dst, :] = k_ref[0, src, :]
            tmp[2, dst, :] = v_ref[0, src, :]
            return cr

        lax.fori_loop(0, n1 // ATT_STAGE, dense, 0, unroll=2)
        for b in range(ATT_SPLIT):
            def cast(ch, cr, a=a, b=b):
                src = pl.ds(b + ch * (ATT_STAGE * ATT_SPLIT), ATT_STAGE, stride=ATT_SPLIT)
                dst = pl.ds(pl.multiple_of((a + ATT_SPLIT * b) * n2 + ch * ATT_STAGE, ATT_STAGE), ATT_STAGE)
                qs[dst, :] = tmp[0, src, :].astype(BF16)
                ks[dst, :] = tmp[1, src, :].astype(BF16)
                vs[dst, :] = tmp[2, src, :].astype(BF16)
                return cr

            lax.fori_loop(0, n2 // ATT_STAGE, cast, 0)


def _attn_body(q_ref, k_ref, v_ref, bias_ref, *rest, seq_len):
    o_ref, lse_scr, qs, ks, vs, tmp = rest
    order = sorted(range(len(DILATED_CONFIGS)), key=lambda c: -DILATED_CONFIGS[c][1])
    for step, c in enumerate(order):
        d = DILATED_CONFIGS[c][1]
        sub_len = seq_len // d
        first, last = step == 0, step == len(order) - 1

        nstage = sub_len // ATT_STAGE
        nt = sub_len // ATT_TQ

        def stage(n, cr, d=d, sub_len=sub_len, nstage=nstage):
            r = lax.div(n, nstage)
            ch = lax.rem(n, nstage)
            dst = pl.ds(pl.multiple_of(n * ATT_STAGE, ATT_STAGE), ATT_STAGE)
            src = _strided_rows(r + ch * (ATT_STAGE * d), ATT_STAGE, d)
            qs[dst, :] = q_ref[0, src, :].astype(BF16)
            ks[dst, :] = k_ref[0, src, :].astype(BF16)
            vs[dst, :] = v_ref[0, src, :].astype(BF16)
            return cr

        if d == ATT_SPLIT * ATT_SPLIT:
            _stage_two_level(q_ref, k_ref, v_ref, tmp, qs, ks, vs, seq_len)
        else:
            lax.fori_loop(0, d * nstage, stage, 0, unroll=2)

        def tile(n, ct, c=c, d=d, sub_len=sub_len, nt=nt, first=first, last=last):
            r = lax.div(n, nt)
            l0 = lax.rem(n, nt) * ATT_TQ
            k0 = jnp.clip(l0 - ATT_HALF, 0, sub_len - ATT_TK)
            var = lax.div(l0 - k0, ATT_HALF)
            base = r * sub_len
            q = qs[pl.ds(pl.multiple_of(base + l0, ATT_TQ), ATT_TQ), :]
            k = ks[pl.ds(pl.multiple_of(base + k0, ATT_HALF), ATT_TK), :]
            v = vs[pl.ds(pl.multiple_of(base + k0, ATT_HALF), ATT_TK), :]
            s = lax.dot_general(q, k, (((1,), (1,)), ((), ())), preferred_element_type=F32)
            s = s + bias_ref[0, 3 * c + var]
            m = jnp.max(s, axis=-1, keepdims=True)
            p = jnp.exp2(s - m)
            l = jnp.sum(p, axis=-1, keepdims=True)
            o = jnp.dot(p.astype(BF16), v, preferred_element_type=F32)
            mb = jnp.broadcast_to(m, (ATT_TQ, LANES))
            lb = jnp.broadcast_to(l, (ATT_TQ, LANES))
            rows = _strided_rows(r + l0 * d, ATT_TQ, d)
            if first:
                o_ref[rows, :] = o / lb
                lse_scr[rows, :] = mb + jnp.log2(lb)
            else:
                la = lse_scr[rows, :]
                mx = jnp.maximum(la, mb)
                wa = jnp.exp2(la - mx)
                wb = jnp.exp2(mb - mx)
                den = wa + wb * lb
                o_ref[rows, :] = (wa * o_ref[rows, :] + wb * o) / den
                if not last:
                    lse_scr[rows, :] = mx + jnp.log2(den)
            return ct

        lax.fori_loop(0, d * nt, tile, 0, unroll=ATT_UNROLL)


def _attention(qkv, bias, lay, *, group):
    batch, seq, dec_batch, dec_seq = lay
    if group == 0:
        nb, s_len, blk0 = batch, seq, 0
    else:
        assert (batch * seq) % dec_seq == 0
        nb, s_len, blk0 = dec_batch, dec_seq, batch * seq // dec_seq
    assert s_len % (ATT_TK * DILATED_CONFIGS[-1][1]) == 0

    def hm(part):
        return lambda b, h: (part * N_HEADS + h, blk0 + b, 0)

    return pl.pallas_call(
        functools.partial(_attn_body, seq_len=s_len),
        grid=(nb, N_HEADS),
        in_specs=[
            pl.BlockSpec((1, s_len, HEAD_DIM), hm(0)),
            pl.BlockSpec((1, s_len, HEAD_DIM), hm(1)),
            pl.BlockSpec((1, s_len, HEAD_DIM), hm(2)),
            pl.BlockSpec((1, bias.shape[1], ATT_TQ, ATT_TK), lambda b, h: (h, 0, 0, 0)),
        ],
        out_specs=pl.BlockSpec((s_len, HEAD_DIM), lambda b, h: (b, h)),
        out_shape=jax.ShapeDtypeStruct((nb * s_len, ATTN_WIDTH), F32),
        scratch_shapes=([pltpu.VMEM((s_len, LANES), F32)] + [pltpu.VMEM((s_len, HEAD_DIM), BF16)] * 3
                        + [pltpu.VMEM((3, s_len // ATT_SPLIT, HEAD_DIM), F32)]),
        compiler_params=_params(("arbitrary", "arbitrary")),
    )(qkv, qkv, qkv, bias)


def _conv_body(prev_ref, cur_ref, next_ref, w_ref, cb_ref, lg_ref, lb_ref, og_ref, o_ref,
               pad_scr, cv_scr, *, lay, ts):
    batch, seq, dec_batch, dec_seq = lay
    n_prompt = batch * seq
    row0 = pl.program_id(0) * ts
    row1 = row0 + ts
    pos0 = jnp.where(row0 < n_prompt, row0 % seq, (row0 - n_prompt) % dec_seq)
    pos1 = jnp.where(row1 <= n_prompt, row1 % seq, (row1 - n_prompt) % dec_seq)
    nch = CONV_CH // LANES
    lanes = [slice(cc * LANES, (cc + 1) * LANES) for cc in range(nch)]
    for cc in range(nch):
        pad_scr[cc, 0:CONV_HALO, :] = jnp.where(pos0 != 0, prev_ref[:, lanes[cc]], 0.0)
        pad_scr[cc, CONV_HALO:CONV_HALO + ts, :] = cur_ref[:, lanes[cc]]
        pad_scr[cc, CONV_HALO + ts:, :] = jnp.where(pos1 != 0, next_ref[:, lanes[cc]], 0.0)

    rc = 128
    base = CONV_HALO - CONV_WIDTH // 2

    def lane_chunk(cc, carry):
        for r in range(ts // rc):
            for par in range(2):
                acc = jnp.zeros((rc // 2, LANES), F32)
                for j in range(CONV_WIDTH):
                    acc = acc + (pad_scr[cc, pl.ds(base + r * rc + par + j, rc // 2, stride=2), :]
                                 * w_ref[cc, pl.ds(j, 1), :])
                cv_scr[cc, pl.ds(r * rc + par, rc // 2, stride=2), :] = acc
        return carry

    lax.fori_loop(0, nch, lane_chunk, 0)

    rn = 32
    inv_c = 1.0 / CONV_CH

    def row_chunk(r, carry):
        r0 = pl.multiple_of(r * rn, rn)
        u = [cv_scr[cc, pl.ds(r0, rn), :] + cb_ref[:, lanes[cc]] for cc in range(nch)]
        mu = jnp.sum(sum(u), axis=-1, keepdims=True) * inv_c
        u = [a - mu for a in u]
        var = jnp.sum(sum(a * a for a in u), axis=-1, keepdims=True) * inv_c
        rstd = lax.rsqrt(var + NORM_EPS)
        y = [a * rstd * lg_ref[:, lanes[cc]] + lb_ref[:, lanes[cc]] for cc, a in enumerate(u)]
        y = [a * jax.nn.sigmoid(a) for a in y]
        ms = jnp.sum(sum(a * a for a in y), axis=-1, keepdims=True) * inv_c
        rr = lax.rsqrt(ms + NORM_EPS)
        for cc, a in enumerate(y):
            o_ref[pl.ds(r0, rn), lanes[cc]] = (a * rr * og_ref[:, lanes[cc]]).astype(o_ref.dtype)
        return carry

    lax.fori_loop(0, ts // rn, row_chunk, 0, unroll=4)


def _conv(u, conv_w, conv_b, ln_g, ln_b, out_g, lay, ts):
    t = u.shape[0]
    hb = ts // CONV_HALO
    nhb = t // CONV_HALO
    nch = CONV_CH // LANES
    w_slabs = conv_w.reshape(CONV_WIDTH, nch, LANES).transpose(1, 0, 2)
    row = lambda a: a.reshape(1, CONV_CH)
    return pl.pallas_call(
        functools.partial(_conv_body, lay=lay, ts=ts),
        grid=(t // ts,),
        in_specs=[
            pl.BlockSpec((CONV_HALO, CONV_CH), lambda i: (jnp.maximum(i * hb - 1, 0), 0)),
            pl.BlockSpec((ts, CONV_CH), lambda i: (i, 0)),
            pl.BlockSpec((CONV_HALO, CONV_CH), lambda i: (jnp.minimum((i + 1) * hb, nhb - 1), 0)),
            pl.BlockSpec((nch, CONV_WIDTH, LANES), lambda i: (0, 0, 0)),
            pl.BlockSpec((1, CONV_CH), lambda i: (0, 0)),
            pl.BlockSpec((1, CONV_CH), lambda i: (0, 0)),
            pl.BlockSpec((1, CONV_CH), lambda i: (0, 0)),
            pl.BlockSpec((1, CONV_CH), lambda i: (0, 0)),
        ],
        out_specs=pl.BlockSpec((ts, CONV_CH), lambda i: (i, 0)),
        out_shape=jax.ShapeDtypeStruct((t, CONV_CH), BF16),
        scratch_shapes=[pltpu.VMEM((nch, ts + 2 * CONV_HALO, LANES), F32),
                        pltpu.VMEM((nch, ts, LANES), F32)],
        compiler_params=_params(("arbitrary",)),
    )(u, u, u, w_slabs, row(conv_b), row(ln_g), row(ln_b), row(out_g))


def _proj_out_body(attn_p_ref, attn_s_ref, conv_ref, x_ref, ag_ref, wout_ref, gate_ref, g2_ref, shift_ref,
                   scale_ref, wr_ref, br_ref, xmid_ref, h_ref, ri_ref, rf_ref, *, n_prompt_blocks):
    for s in range(attn_p_ref.shape[0] // PROJ_OUT_SUB):
        _proj_out_rows(slice(s * PROJ_OUT_SUB, (s + 1) * PROJ_OUT_SUB), attn_p_ref, attn_s_ref, conv_ref,
                       x_ref, ag_ref, wout_ref, gate_ref, g2_ref, shift_ref, scale_ref, wr_ref, br_ref,
                       xmid_ref, h_ref, ri_ref, rf_ref, n_prompt_blocks)


def _proj_out_rows(rs, attn_p_ref, attn_s_ref, conv_ref, x_ref, ag_ref, wout_ref, gate_ref, g2_ref, shift_ref,
                   scale_ref, wr_ref, br_ref, xmid_ref, h_ref, ri_ref, rf_ref, n_prompt_blocks):
    a = jnp.where(pl.program_id(0) < n_prompt_blocks, attn_p_ref[rs, :], attn_s_ref[rs, :])
    ms = jnp.mean(a * a, axis=-1, keepdims=True)
    an = (a * lax.rsqrt(ms + NORM_EPS) * ag_ref[...]).astype(BF16)
    o = jnp.dot(an, wout_ref[0:ATTN_WIDTH, :], preferred_element_type=F32)
    o = o + jnp.dot(conv_ref[rs, :], wout_ref[ATTN_WIDTH:, :], preferred_element_type=F32)
    x = x_ref[rs, :] + gate_ref[0] * o
    xmid_ref[rs, :] = x
    hb = _norm_mod(x, g2_ref, scale_ref, shift_ref)
    h_ref[rs, :] = _pack_rows(hb)

    lg = jnp.dot(hb, wr_ref[...], preferred_element_type=F32) + br_ref[...]
    lane = lax.broadcasted_iota(jnp.int32, lg.shape, 1)
    lane_f = lane.astype(F32)
    neg = jnp.float32(-jnp.inf)
    big = jnp.float32(ROUTE_LANES)
    gmask = lane < N_GROUPS
    lgm = jnp.where(gmask, lg, neg)
    gmax = jnp.max(lgm, axis=-1, keepdims=True)
    gidx = jnp.min(jnp.where(lgm == gmax, lane_f, big), axis=-1, keepdims=True).astype(jnp.int32)
    gsum = jnp.sum(jnp.where(gmask, jnp.exp(lg - gmax), 0.0), axis=-1, keepdims=True)
    pg = 1.0 / gsum
    lo = N_GROUPS + gidx * EXPERTS_PER_GROUP
    emask = (lane >= lo) & (lane < lo + EXPERTS_PER_GROUP)
    le1 = jnp.where(emask, lg, neg)
    e1 = jnp.max(le1, axis=-1, keepdims=True)
    i1 = jnp.min(jnp.where(le1 == e1, lane_f, big), axis=-1, keepdims=True)
    le2 = jnp.where(lane_f == i1, neg, le1)
    e2 = jnp.max(le2, axis=-1, keepdims=True)
    i2 = jnp.min(jnp.where(le2 == e2, lane_f, big), axis=-1, keepdims=True)
    r = jnp.exp(e2 - e1)
    w1 = pg / (1.0 + r)
    w2 = pg * r / (1.0 + r)
    id1 = (i1 - N_GROUPS).astype(jnp.int32)
    id2 = (i2 - N_GROUPS).astype(jnp.int32)
    ri_ref[rs, :] = jnp.where(lane == 0, id1, jnp.where(lane == 1, id2, 0))
    rf_ref[rs, :] = jnp.where(lane == 0, w1, jnp.where(lane == 1, w2, 0.0))


def _proj_out(attn_p, attn_s, convn, x, modr, attn_g, w_out_bf, norm2_g, wr, br, lay, tm):
    t = x.shape[0]
    npb = attn_p.shape[0] // tm

    def seq(i):
        return _seq_of_row(i * tm, lay)

    row = lambda a, n: a.reshape(1, n)
    return pl.pallas_call(
        functools.partial(_proj_out_body, n_prompt_blocks=npb),
        grid=(t // tm,),
        in_specs=[
            pl.BlockSpec((tm, ATTN_WIDTH), lambda i: (jnp.minimum(i, npb - 1), 0)),
            pl.BlockSpec((tm, ATTN_WIDTH), lambda i: (jnp.maximum(i - npb, 0), 0)),
            pl.BlockSpec((tm, CONV_CH), lambda i: (i, 0)),
            pl.BlockSpec((tm, D_MODEL), lambda i: (i, 0)),
            pl.BlockSpec((1, ATTN_WIDTH), lambda i: (0, 0)),
            pl.BlockSpec((D_MODEL, D_MODEL), lambda i: (0, 0), pipeline_mode=pl.Buffered(1)),
            pl.BlockSpec((1, 1, D_MODEL), lambda i: (seq(i) * 6 + 2, 0, 0)),
            pl.BlockSpec((1, D_MODEL), lambda i: (0, 0)),
            pl.BlockSpec((1, 1, D_MODEL), lambda i: (seq(i) * 6 + 3, 0, 0)),
            pl.BlockSpec((1, 1, D_MODEL), lambda i: (seq(i) * 6 + 4, 0, 0)),
            pl.BlockSpec((D_MODEL, ROUTE_LANES), lambda i: (0, 0)),
            pl.BlockSpec((1, ROUTE_LANES), lambda i: (0, 0)),
        ],
        out_specs=[
            pl.BlockSpec((tm, D_MODEL), lambda i: (i, 0)),
            pl.BlockSpec((tm, D_MODEL // 2), lambda i: (i, 0)),
            pl.BlockSpec((tm, ROUTE_LANES), lambda i: (i, 0)),
            pl.BlockSpec((tm, ROUTE_LANES), lambda i: (i, 0)),
        ],
        out_shape=[
            jax.ShapeDtypeStruct((t, D_MODEL), F32),
            jax.ShapeDtypeStruct((t, D_MODEL // 2), jnp.uint32),
            jax.ShapeDtypeStruct((t, ROUTE_LANES), jnp.int32),
            jax.ShapeDtypeStruct((t, ROUTE_LANES), F32),
        ],
        compiler_params=_params(("arbitrary",)),
    )(attn_p, attn_s, convn, x, row(attn_g, ATTN_WIDTH), w_out_bf, modr, row(norm2_g, D_MODEL), modr, modr,
      wr, br)


def _dispatch(experts, tm):
    t = experts.shape[0]
    a = t * TOP_K
    nb = -(-(a + N_EXPERTS * (tm - 1)) // tm)
    flat_e = experts.reshape(-1)
    order = jnp.argsort(flat_e).astype(jnp.int32)
    rank = jnp.argsort(order).astype(jnp.int32)
    counts = jnp.sum(flat_e[:, None] == jnp.arange(N_EXPERTS)[None, :], axis=0).astype(jnp.int32)
    starts = jnp.cumsum(counts) - counts
    padded = ((counts + tm - 1) // tm) * tm
    padded_ends = jnp.cumsum(padded)
    padded_starts = padded_ends - padded
    shift = (padded_starts - starts).astype(jnp.int32)
    pos = (rank + shift[flat_e]).reshape(t, TOP_K)
    block_expert = jnp.minimum(
        jnp.sum(padded_ends[None, :] <= (jnp.arange(nb) * tm)[:, None], axis=1), N_EXPERTS - 1).astype(jnp.int32)
    row = jnp.arange(nb * tm, dtype=jnp.int32)
    row_e = jnp.repeat(block_expert, tm)
    sorted_pos = row - shift[row_e]
    valid = sorted_pos < (starts + counts)[row_e]
    src = order[jnp.clip(sorted_pos, 0, a - 1)]
    row_tok = jnp.where(valid, src // TOP_K, row % t).astype(jnp.int32)
    n_valid = (padded_ends[-1] // tm).astype(jnp.int32).reshape(1)
    nonempty = counts > 0
    slot = ((jnp.cumsum(nonempty) - nonempty) % 2).astype(jnp.int32)
    ids = jnp.where(nonempty, jnp.arange(N_EXPERTS), N_EXPERTS)
    nxt = lax.cummin(jnp.concatenate([ids[1:], jnp.array([N_EXPERTS])]), reverse=True)
    nxt = jnp.where(nxt >= N_EXPERTS, -1, nxt).astype(jnp.int32)
    run_info = jnp.concatenate([slot, nxt, (padded_starts // tm).astype(jnp.int32)])
    return row_tok, pos, block_expert, n_valid, run_info


def _expert_weights(be_ref, ri_ref, hbm_refs, bufs, sem, layer, tn):
    j, i = pl.program_id(0), pl.program_id(1)
    e = be_ref[i]
    slot, nxt, first_blk = ri_ref[e], ri_ref[N_EXPERTS + e], ri_ref[2 * N_EXPERTS + e]

    def copies(expert, s):
        return [pltpu.make_async_copy(h.at[layer, expert, :, pl.ds(pl.multiple_of(j * tn, tn), tn)], b.at[s],
                                      sem.at[n, s])
                for n, (h, b) in enumerate(zip(hbm_refs, bufs))]

    @pl.when(i == first_blk)
    def _():
        @pl.when(i == 0)
        def _():
            for c in copies(e, slot):
                c.start()

        for c in copies(e, slot):
            c.wait()

        @pl.when(nxt >= 0)
        def _():
            for c in copies(nxt, 1 - slot):
                c.start()

    return slot


def _gate_up_body(be_ref, nv_ref, ri_ref, xs_ref, wg_hbm, wu_hbm, a_ref, wg_buf, wu_buf, sem, *, layer, tn):
    i = pl.program_id(1)
    half = D_MODEL // 2

    @pl.when(i < nv_ref[0])
    def _():
        slot = _expert_weights(be_ref, ri_ref, (wg_hbm, wu_hbm), (wg_buf, wu_buf), sem, layer, tn)
        lo, hi = (v.astype(BF16) for v in _unpack_rows(xs_ref[...]))
        g = (jnp.dot(lo, wg_buf[slot, :half, :].astype(BF16), preferred_element_type=F32)
             + jnp.dot(hi, wg_buf[slot, half:, :].astype(BF16), preferred_element_type=F32))
        u = (jnp.dot(lo, wu_buf[slot, :half, :].astype(BF16), preferred_element_type=F32)
             + jnp.dot(hi, wu_buf[slot, half:, :].astype(BF16), preferred_element_type=F32))
        a_ref[...] = (g * jax.nn.sigmoid(g) * u).astype(a_ref.dtype)

    @pl.when(i >= nv_ref[0])
    def _():
        a_ref[...] = jnp.zeros_like(a_ref)


def _gate_up(xs, e_gate, e_up, block_expert, n_valid, run_info, layer, tm):
    p = xs.shape[0]
    tn = D_EXPERT
    return pl.pallas_call(
        functools.partial(_gate_up_body, layer=layer, tn=tn),
        grid_spec=pltpu.PrefetchScalarGridSpec(
            num_scalar_prefetch=3,
            grid=(D_EXPERT // tn, p // tm),
            in_specs=[
                pl.BlockSpec((tm, D_MODEL // 2), lambda j, i, be, nv, ri: (jnp.minimum(i, nv[0] - 1), 0)),
                pl.BlockSpec(memory_space=pl.ANY),
                pl.BlockSpec(memory_space=pl.ANY),
            ],
            out_specs=pl.BlockSpec((tm, tn), lambda j, i, be, nv, ri: (i, j)),
            scratch_shapes=[pltpu.VMEM((2, D_MODEL, tn), F32), pltpu.VMEM((2, D_MODEL, tn), F32),
                            pltpu.SemaphoreType.DMA((2, 2))],
        ),
        out_shape=jax.ShapeDtypeStruct((p, D_EXPERT), BF16),
        compiler_params=_params(("arbitrary", "arbitrary")),
    )(block_expert, n_valid, run_info, xs, e_gate, e_up)


def _down_body(be_ref, nv_ref, ri_ref, a_ref, wd_hbm, y_ref, wd_buf, sem, *, layer, tn):
    i = pl.program_id(1)

    @pl.when(i < nv_ref[0])
    def _():
        slot = _expert_weights(be_ref, ri_ref, (wd_hbm,), (wd_buf,), sem, layer, tn)
        y_ref[...] = _pack_rows(jnp.dot(a_ref[...], wd_buf[slot].astype(BF16), preferred_element_type=F32))

    @pl.when(i >= nv_ref[0])
    def _():
        y_ref[...] = jnp.zeros_like(y_ref)


def _down(a, e_down, block_expert, n_valid, run_info, layer, tm):
    p = a.shape[0]
    tn = D_MODEL
    return pl.pallas_call(
        functools.partial(_down_body, layer=layer, tn=tn),
        grid_spec=pltpu.PrefetchScalarGridSpec(
            num_scalar_prefetch=3,
            grid=(D_MODEL // tn, p // tm),
            in_specs=[
                pl.BlockSpec((tm, D_EXPERT), lambda j, i, be, nv, ri: (jnp.minimum(i, nv[0] - 1), 0)),
                pl.BlockSpec(memory_space=pl.ANY),
            ],
            out_specs=pl.BlockSpec((tm, tn // 2), lambda j, i, be, nv, ri: (i, j)),
            scratch_shapes=[pltpu.VMEM((2, D_EXPERT, tn), F32), pltpu.SemaphoreType.DMA((1, 2))],
        ),
        out_shape=jax.ShapeDtypeStruct((p, D_MODEL // 2), jnp.uint32),
        compiler_params=_params(("arbitrary", "arbitrary")),
    )(block_expert, n_valid, run_info, a, e_down)


def _moe_residual(x_ref, y0_ref, y1_ref, rf_ref, gate_ref):
    rf = rf_ref[...]
    y0 = jnp.concatenate(_unpack_rows(y0_ref[...]), axis=1)
    y1 = jnp.concatenate(_unpack_rows(y1_ref[...]), axis=1)
    return x_ref[...] + gate_ref[0] * (rf[:, 0:1] * y0 + rf[:, 1:2] * y1)


def _combine_body(x_ref, y0_ref, y1_ref, rf_ref, gate_ref, shift_ref, scale_ref, g_ref, o_ref, h_ref):
    x = _moe_residual(x_ref, y0_ref, y1_ref, rf_ref, gate_ref)
    o_ref[...] = x
    h_ref[...] = _norm_mod(x, g_ref, scale_ref, shift_ref)


def _combine(x, y0, y1, rf, modr, modr_next, norm_g_next, lay, tm):
    t = x.shape[0]
    blk = pl.BlockSpec((tm, D_MODEL), lambda i: (i, 0))
    yblk = pl.BlockSpec((tm, D_MODEL // 2), lambda i: (i, 0))

    def mod_row(k):
        return pl.BlockSpec((1, 1, D_MODEL), lambda i: (_seq_of_row(i * tm, lay) * 6 + k, 0, 0))

    return pl.pallas_call(
        _combine_body,
        grid=(t // tm,),
        in_specs=[blk, yblk, yblk, pl.BlockSpec((tm, ROUTE_LANES), lambda i: (i, 0)),
                  mod_row(5), mod_row(0), mod_row(1), pl.BlockSpec((1, D_MODEL), lambda i: (0, 0))],
        out_specs=[blk, blk],
        out_shape=[jax.ShapeDtypeStruct((t, D_MODEL), F32), jax.ShapeDtypeStruct((t, D_MODEL), BF16)],
        compiler_params=_params(("arbitrary",)),
    )(x, y0, y1, rf, modr, modr_next, modr_next, norm_g_next.reshape(1, D_MODEL))


def _final_body(x_ref, y0_ref, y1_ref, rf_ref, gate_ref, g_ref, op_ref, os_ref, *, n_prompt_blocks):
    i = pl.program_id(0)
    x = _moe_residual(x_ref, y0_ref, y1_ref, rf_ref, gate_ref)
    ms = jnp.mean(x * x, axis=-1, keepdims=True)
    y = x * lax.rsqrt(ms + NORM_EPS) * g_ref[...]

    @pl.when(i < n_prompt_blocks)
    def _():
        op_ref[...] = y

    @pl.when(i >= n_prompt_blocks)
    def _():
        os_ref[...] = y


def _final(x, y0, y1, rf, modr, final_g, lay, tm):
    batch, seq, dec_batch, dec_seq = lay
    t = x.shape[0]
    n_prompt = batch * seq
    npb = n_prompt // tm
    blk = pl.BlockSpec((tm, D_MODEL), lambda i: (i, 0))
    yblk = pl.BlockSpec((tm, D_MODEL // 2), lambda i: (i, 0))
    return pl.pallas_call(
        functools.partial(_final_body, n_prompt_blocks=npb),
        grid=(t // tm,),
        in_specs=[blk, yblk, yblk, pl.BlockSpec((tm, ROUTE_LANES), lambda i: (i, 0)),
                  pl.BlockSpec((1, 1, D_MODEL), lambda i: (_seq_of_row(i * tm, lay) * 6 + 5, 0, 0)),
                  pl.BlockSpec((1, D_MODEL), lambda i: (0, 0))],
        out_specs=[
            pl.BlockSpec((tm, D_MODEL), lambda i: (jnp.minimum(i, npb - 1), 0)),
            pl.BlockSpec((tm, D_MODEL), lambda i: (jnp.maximum(i - npb, 0), 0)),
        ],
        out_shape=[
            jax.ShapeDtypeStruct((n_prompt, D_MODEL), F32),
            jax.ShapeDtypeStruct((t - n_prompt, D_MODEL), F32),
        ],
        compiler_params=_params(("arbitrary",)),
    )(x, y0, y1, rf, modr, final_g.reshape(1, D_MODEL))


def _trunk(x_prompt, x_sample, c_all, lay, ada_w, ada_b, norm1_g, w_in, conv_w, conv_b, conv_ln_g, conv_ln_b,
           attn_out_g, conv_out_g, w_out, norm2_g, rg_w, rg_b, re_w, re_b, e_gate, e_up, e_down,
           final_g, *, tm_in=1024, tm_out=512, ts_conv=512, tm_moe=512, tm_res=512):
    batch, seq, dec_batch, dec_seq = lay
    ns = c_all.shape[0]
    ns8 = -(-ns // 8) * 8
    c_pad = jnp.zeros((ns8, D_MODEL), F32).at[:ns].set(c_all)
    mod = _ada_mod(c_pad, ada_w, ada_b)
    modrs = [mod[l, :ns].reshape(ns * 6, 1, D_MODEL) for l in range(DEPTH)]
    bias = _attn_bias()
    n_route = N_GROUPS + N_EXPERTS
    y_prompt = y_sample = None
    x, h = _prologue(x_prompt, x_sample, modrs[0], norm1_g[0], lay, tm_res)
    for l in range(DEPTH):
        modr = modrs[l]
        qkv, u = _proj_in(h, w_in[l].astype(BF16), tm_in)
        attn_p = _attention(qkv, bias, lay, group=0)
        attn_s = _attention(qkv, bias, lay, group=1)
        convn = _conv(u, conv_w[l], conv_b[l], conv_ln_g[l], conv_ln_b[l], conv_out_g[l], lay, ts_conv)
        wr = jnp.zeros((D_MODEL, ROUTE_LANES), F32).at[:, :n_route].set(
            jnp.concatenate([rg_w[l], re_w[l]], axis=1)).astype(BF16)
        br = jnp.zeros((1, ROUTE_LANES), F32).at[0, :n_route].set(jnp.concatenate([rg_b[l], re_b[l]]))
        x_mid, h2, ri, rf = _proj_out(attn_p, attn_s, convn, x, modr, attn_out_g[l], w_out[l].astype(BF16),
                                      norm2_g[l], wr, br, lay, tm_out)
        row_tok, pos, block_expert, n_valid, run_info = _dispatch(ri[:, :TOP_K], tm_moe)
        xs = h2.at[row_tok].get(mode='promise_in_bounds')
        act = _gate_up(xs, e_gate, e_up, block_expert, n_valid, run_info, l, tm_moe)
        ys = _down(act, e_down, block_expert, n_valid, run_info, l, tm_moe)
        y0 = ys.at[pos[:, 0]].get(mode='promise_in_bounds')
        y1 = ys.at[pos[:, 1]].get(mode='promise_in_bounds')
        if l < DEPTH - 1:
            x, h = _combine(x_mid, y0, y1, rf, modr, modrs[l + 1], norm1_g[l + 1], lay, tm_res)
        else:
            y_prompt, y_sample = _final(x_mid, y0, y1, rf, modr, final_g, lay, tm_res)
    return (y_prompt.reshape(batch, seq, D_MODEL), y_sample.reshape(dec_batch, dec_seq, D_MODEL))


def kernel(x_prompt, x_sample, c_prompt, c_sample, ada_w, ada_b, norm1_g, w_in, conv_w, conv_b, conv_ln_g, conv_ln_b, attn_out_g, conv_out_g, w_out, norm2_g, rg_w, rg_b, re_w, re_b, e_gate, e_up, e_down, final_g):
    batch, seq, _ = x_prompt.shape
    dec_batch, dec_seq, _ = x_sample.shape
    lay = (batch, seq, dec_batch, dec_seq)
    c_all = jnp.concatenate([c_prompt, c_sample], axis=0)
    return _trunk(x_prompt.reshape(batch * seq, D_MODEL), x_sample.reshape(dec_batch * dec_seq, D_MODEL),
                  c_all, lay, ada_w, ada_b, norm1_g, w_in, conv_w, conv_b, conv_ln_g, conv_ln_b,
                  attn_out_g, conv_out_g, w_out, norm2_g, rg_w, rg_b, re_w, re_b,
                  e_gate, e_up, e_down, final_g)
```

```python
import functools
import math

import jax
import jax.numpy as jnp
from jax import lax
from jax.experimental import pallas as pl
from jax.experimental.pallas import tpu as pltpu

D_MODEL = 2048
DEPTH = 4
N_HEADS = 8
HEAD_DIM = 128
ATTN_WIDTH = N_HEADS * HEAD_DIM
CONV_CH = D_MODEL - ATTN_WIDTH
CONV_WIDTH = 31
DILATED_CONFIGS = ((128, 1), (512, 4), (2048, 16))
N_GROUPS = 4
EXPERTS_PER_GROUP = 8
N_EXPERTS = N_GROUPS * EXPERTS_PER_GROUP
TOP_K = 2
D_EXPERT = 1024
NORM_EPS = 1e-6
MASK_VALUE = -1e30
LOG2_E = math.log2(math.e)

LANES = 128
ATT_TQ = 128
ATT_TK = 256
ATT_HALF = 64
ATT_STAGE = 256
ATT_UNROLL = 32
ATT_SPLIT = 4
CONV_HALO = 16
ROUTE_LANES = 128
PROJ_OUT_SUB = 256
VMEM_LIMIT = 56 * 1024 * 1024

F32 = jnp.float32
BF16 = jnp.bfloat16


def _params(sem, vmem=VMEM_LIMIT):
    return pltpu.CompilerParams(dimension_semantics=sem, vmem_limit_bytes=vmem)


def _seq_of_row(row, lay):
    batch, seq, _, dec_seq = lay
    n_prompt = batch * seq
    return jnp.where(row < n_prompt, row // seq, batch + (row - n_prompt) // dec_seq)


def _ada_body(c_ref, w_ref, b_ref, o_ref):
    c = c_ref[...]
    sc = c * jax.nn.sigmoid(c)
    o_ref[0] = jnp.dot(sc, w_ref[0], preferred_element_type=F32) + b_ref[0]


def _ada_mod(c_pad, ada_w, ada_b):
    ns8 = c_pad.shape[0]
    tn = 1024
    nj = 6 * D_MODEL // tn
    return pl.pallas_call(
        _ada_body,
        grid=(DEPTH, nj),
        in_specs=[
            pl.BlockSpec((ns8, D_MODEL), lambda l, j: (0, 0)),
            pl.BlockSpec((1, D_MODEL, tn), lambda l, j: (l, 0, j)),
            pl.BlockSpec((1, 1, tn), lambda l, j: (l, 0, j)),
        ],
        out_specs=pl.BlockSpec((1, ns8, tn), lambda l, j: (l, 0, j)),
        out_shape=jax.ShapeDtypeStruct((DEPTH, ns8, 6 * D_MODEL), F32),
        compiler_params=_params(("arbitrary", "arbitrary")),
    )(c_pad, ada_w, ada_b.reshape(DEPTH, 1, 6 * D_MODEL))


def _pack_rows(v):
    bits = pltpu.bitcast(v.astype(BF16).astype(F32), jnp.uint32)
    half = v.shape[1] // 2
    return (bits[:, :half] >> 16) | (bits[:, half:] & jnp.uint32(0xFFFF0000))


def _unpack_rows(words):
    return (pltpu.bitcast(words << 16, F32), pltpu.bitcast(words & jnp.uint32(0xFFFF0000), F32))


def _norm_mod(x, g_ref, scale_ref, shift_ref):
    ms = jnp.mean(x * x, axis=-1, keepdims=True)
    y = x * lax.rsqrt(ms + NORM_EPS) * g_ref[...]
    return (y * (1.0 + scale_ref[0]) + shift_ref[0]).astype(BF16)


def _prologue_body(xp_ref, xs_ref, shift_ref, scale_ref, g_ref, x_ref, h_ref, *, n_prompt_blocks):
    x = jnp.where(pl.program_id(0) < n_prompt_blocks, xp_ref[...], xs_ref[...])
    x_ref[...] = x
    h_ref[...] = _norm_mod(x, g_ref, scale_ref, shift_ref)


def _prologue(x_prompt, x_sample, modr, norm_g, lay, tm):
    n_prompt, n_sample = x_prompt.shape[0], x_sample.shape[0]
    t = n_prompt + n_sample
    npb = n_prompt // tm
    blk = pl.BlockSpec((tm, D_MODEL), lambda i: (i, 0))
    return pl.pallas_call(
        functools.partial(_prologue_body, n_prompt_blocks=npb),
        grid=(t // tm,),
        in_specs=[
            pl.BlockSpec((tm, D_MODEL), lambda i: (jnp.minimum(i, npb - 1), 0)),
            pl.BlockSpec((tm, D_MODEL), lambda i: (jnp.maximum(i - npb, 0), 0)),
            pl.BlockSpec((1, 1, D_MODEL), lambda i: (_seq_of_row(i * tm, lay) * 6 + 0, 0, 0)),
            pl.BlockSpec((1, 1, D_MODEL), lambda i: (_seq_of_row(i * tm, lay) * 6 + 1, 0, 0)),
            pl.BlockSpec((1, D_MODEL), lambda i: (0, 0)),
        ],
        out_specs=[blk, blk],
        out_shape=[jax.ShapeDtypeStruct((t, D_MODEL), F32), jax.ShapeDtypeStruct((t, D_MODEL), BF16)],
        compiler_params=_params(("arbitrary",)),
    )(x_prompt, x_sample, modr, modr, norm_g.reshape(1, D_MODEL))


def _proj_in_body(h_ref, wqkv_ref, wga_ref, wgb_ref, qkv_ref, u_ref):
    j = pl.program_id(1)

    @pl.when(j < 3)
    def _():
        r = jnp.dot(h_ref[...], wqkv_ref[...], preferred_element_type=F32)
        r = r * jnp.where(j == 0, HEAD_DIM ** -0.5 * LOG2_E, 1.0).astype(F32)
        for hh in range(N_HEADS):
            qkv_ref[hh] = r[:, hh * HEAD_DIM:(hh + 1) * HEAD_DIM]

    @pl.when(j == 3)
    def _():
        h = h_ref[...]
        a = jnp.dot(h, wga_ref[...], preferred_element_type=F32)
        b = jnp.dot(h, wgb_ref[...], preferred_element_type=F32)
        u_ref[...] = a * jax.nn.sigmoid(b)


def _proj_in(h, w_in_bf, tm):
    t = h.shape[0]
    tn = ATTN_WIDTH
    assert CONV_CH == tn
    resident = dict(pipeline_mode=pl.Buffered(1))
    return pl.pallas_call(
        _proj_in_body,
        grid=(t // tm, 4),
        in_specs=[
            pl.BlockSpec((tm, D_MODEL), lambda i, j: (i, 0)),
            pl.BlockSpec((D_MODEL, tn), lambda i, j: (0, jnp.minimum(j, 2))),
            pl.BlockSpec((D_MODEL, tn), lambda i, j: (0, 3), **resident),
            pl.BlockSpec((D_MODEL, tn), lambda i, j: (0, 4), **resident),
        ],
        out_specs=[
            pl.BlockSpec((N_HEADS, tm, HEAD_DIM), lambda i, j: (jnp.minimum(j, 2), i, 0)),
            pl.BlockSpec((tm, CONV_CH), lambda i, j: (i, 0)),
        ],
        out_shape=[
            jax.ShapeDtypeStruct((3 * N_HEADS, t, HEAD_DIM), F32),
            jax.ShapeDtypeStruct((t, CONV_CH), F32),
        ],
        compiler_params=_params(("arbitrary", "arbitrary")),
    )(h, w_in_bf, w_in_bf, w_in_bf)


def _alibi_slopes():
    return jnp.exp2(-8.0 * jnp.arange(1, N_HEADS + 1, dtype=F32) / N_HEADS)


def _attn_bias():
    i = jnp.arange(ATT_TQ)[:, None]
    j = jnp.arange(ATT_TK)[None, :]
    out = []
    for _, d in DILATED_CONFIGS:
        for v in range(3):
            rel = jnp.abs(j - ATT_HALF * v - i)
            dist = (d * rel).astype(F32)
            b = -_alibi_slopes()[:, None, None] * dist[None]
            out.append(jnp.where((rel <= ATT_HALF)[None], b * LOG2_E, MASK_VALUE))
    return jnp.stack(out, axis=1)


def _strided_rows(start, size, stride):
    return pl.ds(start, size, stride=stride) if stride > 1 else pl.ds(start, size)


def _stage_two_level(q_ref, k_ref, v_ref, tmp, qs, ks, vs, seq_len):
    n1 = seq_len // ATT_SPLIT
    n2 = n1 // ATT_SPLIT
    for a in range(ATT_SPLIT):
        def dense(ch, cr, a=a):
            src = pl.ds(a + ch * (ATT_STAGE * ATT_SPLIT), ATT_STAGE, stride=ATT_SPLIT)
            dst = pl.ds(pl.multiple_of(ch * ATT_STAGE, ATT_STAGE), ATT_STAGE)
            tmp[0, dst, :] = q_ref[0, src, :]
            tmp[1, dst, :] = k_ref[0, src, :]
            tmp[2, dst, :] = v_ref[0, src, :]
            return cr

        lax.fori_loop(0, n1 // ATT_STAGE, dense, 0, unroll=2)
        for b in range(ATT_SPLIT):
            def cast(ch, cr, a=a, b=b):
                src = pl.ds(b + ch * (ATT_STAGE * ATT_SPLIT), ATT_STAGE, stride=ATT_SPLIT)
                dst = pl.ds(pl.multiple_of((a + ATT_SPLIT * b) * n2 + ch * ATT_STAGE, ATT_STAGE), ATT_STAGE)
                qs[dst, :] = tmp[0, src, :].astype(BF16)
                ks[dst, :] = tmp[1, src, :].astype(BF16)
                vs[dst, :] = tmp[2, src, :].astype(BF16)
                return cr

            lax.fori_loop(0, n2 // ATT_STAGE, cast, 0)


def _attn_body(q_ref, k_ref, v_ref, bias_ref, *rest, seq_len):
    o_ref, lse_scr, qs, ks, vs, tmp = rest
    order = sorted(range(len(DILATED_CONFIGS)), key=lambda c: -DILATED_CONFIGS[c][1])
    for step, c in enumerate(order):
        d = DILATED_CONFIGS[c][1]
        sub_len = seq_len // d
        first, last = step == 0, step == len(order) - 1

        nstage = sub_len // ATT_STAGE
        nt = sub_len // ATT_TQ

        def stage(n, cr, d=d, sub_len=sub_len, nstage=nstage):
            r = lax.div(n, nstage)
            ch = lax.rem(n, nstage)
            dst = pl.ds(pl.multiple_of(n * ATT_STAGE, ATT_STAGE), ATT_STAGE)
            src = _strided_rows(r + ch * (ATT_STAGE * d), ATT_STAGE, d)
            qs[dst, :] = q_ref[0, src, :].astype(BF16)
            ks[dst, :] = k_ref[0, src, :].astype(BF16)
            vs[dst, :] = v_ref[0, src, :].astype(BF16)
            return cr

        if d == ATT_SPLIT * ATT_SPLIT:
            _stage_two_level(q_ref, k_ref, v_ref, tmp, qs, ks, vs, seq_len)
        else:
            lax.fori_loop(0, d * nstage, stage, 0, unroll=2)

        def tile(n, ct, c=c, d=d, sub_len=sub_len, nt=nt, first=first, last=last):
            r = lax.div(n, nt)
            l0 = lax.rem(n, nt) * ATT_TQ
            k0 = jnp.clip(l0 - ATT_HALF, 0, sub_len - ATT_TK)
            var = lax.div(l0 - k0, ATT_HALF)
            base = r * sub_len
            q = qs[pl.ds(pl.multiple_of(base + l0, ATT_TQ), ATT_TQ), :]
            k = ks[pl.ds(pl.multiple_of(base + k0, ATT_HALF), ATT_TK), :]
            v = vs[pl.ds(pl.multiple_of(base + k0, ATT_HALF), ATT_TK), :]
            s = lax.dot_general(q, k, (((1,), (1,)), ((), ())), preferred_element_type=F32)
            s = s + bias_ref[0, 3 * c + var]
            m = jnp.max(s, axis=-1, keepdims=True)
            p = jnp.exp2(s - m)
            l = jnp.sum(p, axis=-1, keepdims=True)
            o = jnp.dot(p.astype(BF16), v, preferred_element_type=F32)
            mb = jnp.broadcast_to(m, (ATT_TQ, LANES))
            lb = jnp.broadcast_to(l, (ATT_TQ, LANES))
            rows = _strided_rows(r + l0 * d, ATT_TQ, d)
            if first:
                o_ref[rows, :] = o / lb
                lse_scr[rows, :] = mb + jnp.log2(lb)
            else:
                la = lse_scr[rows, :]
                mx = jnp.maximum(la, mb)
                wa = jnp.exp2(la - mx)
                wb = jnp.exp2(mb - mx)
                den = wa + wb * lb
                o_ref[rows, :] = (wa * o_ref[rows, :] + wb * o) / den
                if not last:
                    lse_scr[rows, :] = mx + jnp.log2(den)
            return ct

        lax.fori_loop(0, d * nt, tile, 0, unroll=ATT_UNROLL)


def _attention(qkv, bias, lay, *, group):
    batch, seq, dec_batch, dec_seq = lay
    if group == 0:
        nb, s_len, blk0 = batch, seq, 0
    else:
        assert (batch * seq) % dec_seq == 0
        nb, s_len, blk0 = dec_batch, dec_seq, batch * seq // dec_seq
    assert s_len % (ATT_TK * DILATED_CONFIGS[-1][1]) == 0

    def hm(part):
        return lambda b, h: (part * N_HEADS + h, blk0 + b, 0)

    return pl.pallas_call(
        functools.partial(_attn_body, seq_len=s_len),
        grid=(nb, N_HEADS),
        in_specs=[
            pl.BlockSpec((1, s_len, HEAD_DIM), hm(0)),
            pl.BlockSpec((1, s_len, HEAD_DIM), hm(1)),
            pl.BlockSpec((1, s_len, HEAD_DIM), hm(2)),
            pl.BlockSpec((1, bias.shape[1], ATT_TQ, ATT_TK), lambda b, h: (h, 0, 0, 0)),
        ],
        out_specs=pl.BlockSpec((s_len, HEAD_DIM), lambda b, h: (b, h)),
        out_shape=jax.ShapeDtypeStruct((nb * s_len, ATTN_WIDTH), F32),
        scratch_shapes=([pltpu.VMEM((s_len, LANES), F32)] + [pltpu.VMEM((s_len, HEAD_DIM), BF16)] * 3
                        + [pltpu.VMEM((3, s_len // ATT_SPLIT, HEAD_DIM), F32)]),
        compiler_params=_params(("arbitrary", "arbitrary")),
    )(qkv, qkv, qkv, bias)


def _conv_body(prev_ref, cur_ref, next_ref, w_ref, cb_ref, lg_ref, lb_ref, og_ref, o_ref,
               pad_scr, cv_scr, *, lay, ts):
    batch, seq, dec_batch, dec_seq = lay
    n_prompt = batch * seq
    row0 = pl.program_id(0) * ts
    row1 = row0 + ts
    pos0 = jnp.where(row0 < n_prompt, row0 % seq, (row0 - n_prompt) % dec_seq)
    pos1 = jnp.where(row1 <= n_prompt, row1 % seq, (row1 - n_prompt) % dec_seq)
    nch = CONV_CH // LANES
    lanes = [slice(cc * LANES, (cc + 1) * LANES) for cc in range(nch)]
    for cc in range(nch):
        pad_scr[cc, 0:CONV_HALO, :] = jnp.where(pos0 != 0, prev_ref[:, lanes[cc]], 0.0)
        pad_scr[cc, CONV_HALO:CONV_HALO + ts, :] = cur_ref[:, lanes[cc]]
        pad_scr[cc, CONV_HALO + ts:, :] = jnp.where(pos1 != 0, next_ref[:, lanes[cc]], 0.0)

    rc = 128
    base = CONV_HALO - CONV_WIDTH // 2

    def lane_chunk(cc, carry):
        for r in range(ts // rc):
            for par in range(2):
                acc = jnp.zeros((rc // 2, LANES), F32)
                for j in range(CONV_WIDTH):
                    acc = acc + (pad_scr[cc, pl.ds(base + r * rc + par + j, rc // 2, stride=2), :]
                                 * w_ref[cc, pl.ds(j, 1), :])
                cv_scr[cc, pl.ds(r * rc + par, rc // 2, stride=2), :] = acc
        return carry

    lax.fori_loop(0, nch, lane_chunk, 0)

    rn = 32
    inv_c = 1.0 / CONV_CH

    def row_chunk(r, carry):
        r0 = pl.multiple_of(r * rn, rn)
        u = [cv_scr[cc, pl.ds(r0, rn), :] + cb_ref[:, lanes[cc]] for cc in range(nch)]
        mu = jnp.sum(sum(u), axis=-1, keepdims=True) * inv_c
        u = [a - mu for a in u]
        var = jnp.sum(sum(a * a for a in u), axis=-1, keepdims=True) * inv_c
        rstd = lax.rsqrt(var + NORM_EPS)
        y = [a * rstd * lg_ref[:, lanes[cc]] + lb_ref[:, lanes[cc]] for cc, a in enumerate(u)]
        y = [a * jax.nn.sigmoid(a) for a in y]
        ms = jnp.sum(sum(a * a for a in y), axis=-1, keepdims=True) * inv_c
        rr = lax.rsqrt(ms + NORM_EPS)
        for cc, a in enumerate(y):
            o_ref[pl.ds(r0, rn), lanes[cc]] = (a * rr * og_ref[:, lanes[cc]]).astype(o_ref.dtype)
        return carry

    lax.fori_loop(0, ts // rn, row_chunk, 0, unroll=4)


def _conv(u, conv_w, conv_b, ln_g, ln_b, out_g, lay, ts):
    t = u.shape[0]
    hb = ts // CONV_HALO
    nhb = t // CONV_HALO
    nch = CONV_CH // LANES
    w_slabs = conv_w.reshape(CONV_WIDTH, nch, LANES).transpose(1, 0, 2)
    row = lambda a: a.reshape(1, CONV_CH)
    return pl.pallas_call(
        functools.partial(_conv_body, lay=lay, ts=ts),
        grid=(t // ts,),
        in_specs=[
            pl.BlockSpec((CONV_HALO, CONV_CH), lambda i: (jnp.maximum(i * hb - 1, 0), 0)),
            pl.BlockSpec((ts, CONV_CH), lambda i: (i, 0)),
            pl.BlockSpec((CONV_HALO, CONV_CH), lambda i: (jnp.minimum((i + 1) * hb, nhb - 1), 0)),
            pl.BlockSpec((nch, CONV_WIDTH, LANES), lambda i: (0, 0, 0)),
            pl.BlockSpec((1, CONV_CH), lambda i: (0, 0)),
            pl.BlockSpec((1, CONV_CH), lambda i: (0, 0)),
            pl.BlockSpec((1, CONV_CH), lambda i: (0, 0)),
            pl.BlockSpec((1, CONV_CH), lambda i: (0, 0)),
        ],
        out_specs=pl.BlockSpec((ts, CONV_CH), lambda i: (i, 0)),
        out_shape=jax.ShapeDtypeStruct((t, CONV_CH), BF16),
        scratch_shapes=[pltpu.VMEM((nch, ts + 2 * CONV_HALO, LANES), F32),
                        pltpu.VMEM((nch, ts, LANES), F32)],
        compiler_params=_params(("arbitrary",)),
    )(u, u, u, w_slabs, row(conv_b), row(ln_g), row(ln_b), row(out_g))


def _proj_out_body(attn_p_ref, attn_s_ref, conv_ref, x_ref, ag_ref, wout_ref, gate_ref, g2_ref, shift_ref,
                   scale_ref, wr_ref, br_ref, xmid_ref, h_ref, ri_ref, rf_ref, cnt_ref, *, n_prompt_blocks):
    @pl.when(pl.program_id(0) == 0)
    def _():
        cnt_ref[...] = jnp.zeros_like(cnt_ref)

    for s in range(attn_p_ref.shape[0] // PROJ_OUT_SUB):
        _proj_out_rows(slice(s * PROJ_OUT_SUB, (s + 1) * PROJ_OUT_SUB), attn_p_ref, attn_s_ref, conv_ref,
                       x_ref, ag_ref, wout_ref, gate_ref, g2_ref, shift_ref, scale_ref, wr_ref, br_ref,
                       xmid_ref, h_ref, ri_ref, rf_ref, cnt_ref, n_prompt_blocks)


def _proj_out_rows(rs, attn_p_ref, attn_s_ref, conv_ref, x_ref, ag_ref, wout_ref, gate_ref, g2_ref, shift_ref,
                   scale_ref, wr_ref, br_ref, xmid_ref, h_ref, ri_ref, rf_ref, cnt_ref, n_prompt_blocks):
    a = jnp.where(pl.program_id(0) < n_prompt_blocks, attn_p_ref[rs, :], attn_s_ref[rs, :])
    ms = jnp.mean(a * a, axis=-1, keepdims=True)
    an = (a * lax.rsqrt(ms + NORM_EPS) * ag_ref[...]).astype(BF16)
    o = jnp.dot(an, wout_ref[0:ATTN_WIDTH, :], preferred_element_type=F32)
    o = o + jnp.dot(conv_ref[rs, :], wout_ref[ATTN_WIDTH:, :], preferred_element_type=F32)
    x = x_ref[rs, :] + gate_ref[0] * o
    xmid_ref[rs, :] = x
    hb = _norm_mod(x, g2_ref, scale_ref, shift_ref)
    h_ref[rs, :] = _pack_rows(hb)

    lg = jnp.dot(hb, wr_ref[...], preferred_element_type=F32) + br_ref[...]
    lane = lax.broadcasted_iota(jnp.int32, lg.shape, 1)
    lane_f = lane.astype(F32)
    neg = jnp.float32(-jnp.inf)
    big = jnp.float32(ROUTE_LANES)
    gmask = lane < N_GROUPS
    lgm = jnp.where(gmask, lg, neg)
    gmax = jnp.max(lgm, axis=-1, keepdims=True)
    gidx = jnp.min(jnp.where(lgm == gmax, lane_f, big), axis=-1, keepdims=True).astype(jnp.int32)
    gsum = jnp.sum(jnp.where(gmask, jnp.exp(lg - gmax), 0.0), axis=-1, keepdims=True)
    pg = 1.0 / gsum
    lo = N_GROUPS + gidx * EXPERTS_PER_GROUP
    emask = (lane >= lo) & (lane < lo + EXPERTS_PER_GROUP)
    le1 = jnp.where(emask, lg, neg)
    e1 = jnp.max(le1, axis=-1, keepdims=True)
    i1 = jnp.min(jnp.where(le1 == e1, lane_f, big), axis=-1, keepdims=True)
    le2 = jnp.where(lane_f == i1, neg, le1)
    e2 = jnp.max(le2, axis=-1, keepdims=True)
    i2 = jnp.min(jnp.where(le2 == e2, lane_f, big), axis=-1, keepdims=True)
    r = jnp.exp(e2 - e1)
    w1 = pg / (1.0 + r)
    w2 = pg * r / (1.0 + r)
    id1 = (i1 - N_GROUPS).astype(jnp.int32)
    id2 = (i2 - N_GROUPS).astype(jnp.int32)
    ri_ref[rs, :] = jnp.where(lane == 0, id1, jnp.where(lane == 1, id2, 0))
    rf_ref[rs, :] = jnp.where(lane == 0, w1, jnp.where(lane == 1, w2, 0.0))
    hit = jnp.where((lane_f == i1) | (lane_f == i2), 1.0, 0.0)
    cnt_ref[0:1, :] = cnt_ref[0:1, :] + jnp.sum(hit, axis=0, keepdims=True)


def _proj_out(attn_p, attn_s, convn, x, modr, attn_g, w_out_bf, norm2_g, wr, br, lay, tm):
    t = x.shape[0]
    npb = attn_p.shape[0] // tm

    def seq(i):
        return _seq_of_row(i * tm, lay)

    row = lambda a, n: a.reshape(1, n)
    return pl.pallas_call(
        functools.partial(_proj_out_body, n_prompt_blocks=npb),
        grid=(t // tm,),
        in_specs=[
            pl.BlockSpec((tm, ATTN_WIDTH), lambda i: (jnp.minimum(i, npb - 1), 0)),
            pl.BlockSpec((tm, ATTN_WIDTH), lambda i: (jnp.maximum(i - npb, 0), 0)),
            pl.BlockSpec((tm, CONV_CH), lambda i: (i, 0)),
            pl.BlockSpec((tm, D_MODEL), lambda i: (i, 0)),
            pl.BlockSpec((1, ATTN_WIDTH), lambda i: (0, 0)),
            pl.BlockSpec((D_MODEL, D_MODEL), lambda i: (0, 0), pipeline_mode=pl.Buffered(1)),
            pl.BlockSpec((1, 1, D_MODEL), lambda i: (seq(i) * 6 + 2, 0, 0)),
            pl.BlockSpec((1, D_MODEL), lambda i: (0, 0)),
            pl.BlockSpec((1, 1, D_MODEL), lambda i: (seq(i) * 6 + 3, 0, 0)),
            pl.BlockSpec((1, 1, D_MODEL), lambda i: (seq(i) * 6 + 4, 0, 0)),
            pl.BlockSpec((D_MODEL, ROUTE_LANES), lambda i: (0, 0)),
            pl.BlockSpec((1, ROUTE_LANES), lambda i: (0, 0)),
        ],
        out_specs=[
            pl.BlockSpec((tm, D_MODEL), lambda i: (i, 0)),
            pl.BlockSpec((tm, D_MODEL // 2), lambda i: (i, 0)),
            pl.BlockSpec((tm, ROUTE_LANES), lambda i: (i, 0)),
            pl.BlockSpec((tm, ROUTE_LANES), lambda i: (i, 0)),
            pl.BlockSpec((8, ROUTE_LANES), lambda i: (0, 0)),
        ],
        out_shape=[
            jax.ShapeDtypeStruct((t, D_MODEL), F32),
            jax.ShapeDtypeStruct((t, D_MODEL // 2), jnp.uint32),
            jax.ShapeDtypeStruct((t, ROUTE_LANES), jnp.int32),
            jax.ShapeDtypeStruct((t, ROUTE_LANES), F32),
            jax.ShapeDtypeStruct((8, ROUTE_LANES), F32),
        ],
        compiler_params=_params(("arbitrary",)),
    )(attn_p, attn_s, convn, x, row(attn_g, ATTN_WIDTH), w_out_bf, modr, row(norm2_g, D_MODEL), modr, modr,
      wr, br)


def _dispatch(experts, counts, tm):
    t = experts.shape[0]
    a = t * TOP_K
    nb = -(-(a + N_EXPERTS * (tm - 1)) // tm)
    flat_e = experts.reshape(-1)
    order = jnp.argsort(flat_e).astype(jnp.int32)
    rank = jnp.argsort(order).astype(jnp.int32)
    starts = jnp.cumsum(counts) - counts
    padded = ((counts + tm - 1) // tm) * tm
    padded_ends = jnp.cumsum(padded)
    padded_starts = padded_ends - padded
    shift = (padded_starts - starts).astype(jnp.int32)
    pos = (rank + shift[flat_e]).reshape(t, TOP_K)
    block_expert = jnp.minimum(
        jnp.sum(padded_ends[None, :] <= (jnp.arange(nb) * tm)[:, None], axis=1), N_EXPERTS - 1).astype(jnp.int32)
    row = jnp.arange(nb * tm, dtype=jnp.int32)
    row_e = jnp.repeat(block_expert, tm)
    sorted_pos = row - shift[row_e]
    valid = sorted_pos < (starts + counts)[row_e]
    src = order[jnp.clip(sorted_pos, 0, a - 1)]
    row_tok = jnp.where(valid, src // TOP_K, row % t).astype(jnp.int32)
    n_valid = (padded_ends[-1] // tm).astype(jnp.int32).reshape(1)
    nonempty = counts > 0
    slot = ((jnp.cumsum(nonempty) - nonempty) % 2).astype(jnp.int32)
    ids = jnp.where(nonempty, jnp.arange(N_EXPERTS), N_EXPERTS)
    nxt = lax.cummin(jnp.concatenate([ids[1:], jnp.array([N_EXPERTS])]), reverse=True)
    nxt = jnp.where(nxt >= N_EXPERTS, -1, nxt).astype(jnp.int32)
    run_info = jnp.concatenate([slot, nxt, (padded_starts // tm).astype(jnp.int32)])
    return row_tok, pos, block_expert, n_valid, run_info


def _expert_weights(be_ref, ri_ref, hbm_refs, bufs, sem, layer, tn):
    j, i = pl.program_id(0), pl.program_id(1)
    e = be_ref[i]
    slot, nxt, first_blk = ri_ref[e], ri_ref[N_EXPERTS + e], ri_ref[2 * N_EXPERTS + e]

    def copies(expert, s):
        return [pltpu.make_async_copy(h.at[layer, expert, :, pl.ds(pl.multiple_of(j * tn, tn), tn)], b.at[s],
                                      sem.at[n, s])
                for n, (h, b) in enumerate(zip(hbm_refs, bufs))]

    @pl.when(i == first_blk)
    def _():
        @pl.when(i == 0)
        def _():
            for c in copies(e, slot):
                c.start()

        for c in copies(e, slot):
            c.wait()

        @pl.when(nxt >= 0)
        def _():
            for c in copies(nxt, 1 - slot):
                c.start()

    return slot


def _gate_up_body(be_ref, nv_ref, ri_ref, xs_ref, wg_hbm, wu_hbm, a_ref, wg_buf, wu_buf, sem, *, layer, tn):
    i = pl.program_id(1)
    half = D_MODEL // 2

    @pl.when(i < nv_ref[0])
    def _():
        slot = _expert_weights(be_ref, ri_ref, (wg_hbm, wu_hbm), (wg_buf, wu_buf), sem, layer, tn)
        lo, hi = (v.astype(BF16) for v in _unpack_rows(xs_ref[...]))
        g = (jnp.dot(lo, wg_buf[slot, :half, :].astype(BF16), preferred_element_type=F32)
             + jnp.dot(hi, wg_buf[slot, half:, :].astype(BF16), preferred_element_type=F32))
        u = (jnp.dot(lo, wu_buf[slot, :half, :].astype(BF16), preferred_element_type=F32)
             + jnp.dot(hi, wu_buf[slot, half:, :].astype(BF16), preferred_element_type=F32))
        a_ref[...] = (g * jax.nn.sigmoid(g) * u).astype(a_ref.dtype)

    @pl.when(i >= nv_ref[0])
    def _():
        a_ref[...] = jnp.zeros_like(a_ref)


def _gate_up(xs, e_gate, e_up, block_expert, n_valid, run_info, layer, tm):
    p = xs.shape[0]
    tn = D_EXPERT
    return pl.pallas_call(
        functools.partial(_gate_up_body, layer=layer, tn=tn),
        grid_spec=pltpu.PrefetchScalarGridSpec(
            num_scalar_prefetch=3,
            grid=(D_EXPERT // tn, p // tm),
            in_specs=[
                pl.BlockSpec((tm, D_MODEL // 2), lambda j, i, be, nv, ri: (jnp.minimum(i, nv[0] - 1), 0)),
                pl.BlockSpec(memory_space=pl.ANY),
                pl.BlockSpec(memory_space=pl.ANY),
            ],
            out_specs=pl.BlockSpec((tm, tn), lambda j, i, be, nv, ri: (i, j)),
            scratch_shapes=[pltpu.VMEM((2, D_MODEL, tn), F32), pltpu.VMEM((2, D_MODEL, tn), F32),
                            pltpu.SemaphoreType.DMA((2, 2))],
        ),
        out_shape=jax.ShapeDtypeStruct((p, D_EXPERT), BF16),
        compiler_params=_params(("arbitrary", "arbitrary")),
    )(block_expert, n_valid, run_info, xs, e_gate, e_up)


def _down_body(be_ref, nv_ref, ri_ref, a_ref, wd_hbm, y_ref, wd_buf, sem, *, layer, tn):
    i = pl.program_id(1)

    @pl.when(i < nv_ref[0])
    def _():
        slot = _expert_weights(be_ref, ri_ref, (wd_hbm,), (wd_buf,), sem, layer, tn)
        y_ref[...] = _pack_rows(jnp.dot(a_ref[...], wd_buf[slot].astype(BF16), preferred_element_type=F32))

    @pl.when(i >= nv_ref[0])
    def _():
        y_ref[...] = jnp.zeros_like(y_ref)


def _down(a, e_down, block_expert, n_valid, run_info, layer, tm):
    p = a.shape[0]
    tn = D_MODEL
    return pl.pallas_call(
        functools.partial(_down_body, layer=layer, tn=tn),
        grid_spec=pltpu.PrefetchScalarGridSpec(
            num_scalar_prefetch=3,
            grid=(D_MODEL // tn, p // tm),
            in_specs=[
                pl.BlockSpec((tm, D_EXPERT), lambda j, i, be, nv, ri: (jnp.minimum(i, nv[0] - 1), 0)),
                pl.BlockSpec(memory_space=pl.ANY),
            ],
            out_specs=pl.BlockSpec((tm, tn // 2), lambda j, i, be, nv, ri: (i, j)),
            scratch_shapes=[pltpu.VMEM((2, D_EXPERT, tn), F32), pltpu.SemaphoreType.DMA((1, 2))],
        ),
        out_shape=jax.ShapeDtypeStruct((p, D_MODEL // 2), jnp.uint32),
        compiler_params=_params(("arbitrary", "arbitrary")),
    )(block_expert, n_valid, run_info, a, e_down)


def _moe_residual(x_ref, y0_ref, y1_ref, rf_ref, gate_ref):
    rf = rf_ref[...]
    y0 = jnp.concatenate(_unpack_rows(y0_ref[...]), axis=1)
    y1 = jnp.concatenate(_unpack_rows(y1_ref[...]), axis=1)
    return x_ref[...] + gate_ref[0] * (rf[:, 0:1] * y0 + rf[:, 1:2] * y1)


def _combine_body(x_ref, y0_ref, y1_ref, rf_ref, gate_ref, shift_ref, scale_ref, g_ref, o_ref, h_ref):
    x = _moe_residual(x_ref, y0_ref, y1_ref, rf_ref, gate_ref)
    o_ref[...] = x
    h_ref[...] = _norm_mod(x, g_ref, scale_ref, shift_ref)


def _combine(x, y0, y1, rf, modr, modr_next, norm_g_next, lay, tm):
    t = x.shape[0]
    blk = pl.BlockSpec((tm, D_MODEL), lambda i: (i, 0))
    yblk = pl.BlockSpec((tm, D_MODEL // 2), lambda i: (i, 0))

    def mod_row(k):
        return pl.BlockSpec((1, 1, D_MODEL), lambda i: (_seq_of_row(i * tm, lay) * 6 + k, 0, 0))

    return pl.pallas_call(
        _combine_body,
        grid=(t // tm,),
        in_specs=[blk, yblk, yblk, pl.BlockSpec((tm, ROUTE_LANES), lambda i: (i, 0)),
                  mod_row(5), mod_row(0), mod_row(1), pl.BlockSpec((1, D_MODEL), lambda i: (0, 0))],
        out_specs=[blk, blk],
        out_shape=[jax.ShapeDtypeStruct((t, D_MODEL), F32), jax.ShapeDtypeStruct((t, D_MODEL), BF16)],
        compiler_params=_params(("arbitrary",)),
    )(x, y0, y1, rf, modr, modr_next, modr_next, norm_g_next.reshape(1, D_MODEL))


def _final_body(x_ref, y0_ref, y1_ref, rf_ref, gate_ref, g_ref, op_ref, os_ref, *, n_prompt_blocks):
    i = pl.program_id(0)
    x = _moe_residual(x_ref, y0_ref, y1_ref, rf_ref, gate_ref)
    ms = jnp.mean(x * x, axis=-1, keepdims=True)
    y = x * lax.rsqrt(ms + NORM_EPS) * g_ref[...]

    @pl.when(i < n_prompt_blocks)
    def _():
        op_ref[...] = y

    @pl.when(i >= n_prompt_blocks)
    def _():
        os_ref[...] = y


def _final(x, y0, y1, rf, modr, final_g, lay, tm):
    batch, seq, dec_batch, dec_seq = lay
    t = x.shape[0]
    n_prompt = batch * seq
    npb = n_prompt // tm
    blk = pl.BlockSpec((tm, D_MODEL), lambda i: (i, 0))
    yblk = pl.BlockSpec((tm, D_MODEL // 2), lambda i: (i, 0))
    return pl.pallas_call(
        functools.partial(_final_body, n_prompt_blocks=npb),
        grid=(t // tm,),
        in_specs=[blk, yblk, yblk, pl.BlockSpec((tm, ROUTE_LANES), lambda i: (i, 0)),
                  pl.BlockSpec((1, 1, D_MODEL), lambda i: (_seq_of_row(i * tm, lay) * 6 + 5, 0, 0)),
                  pl.BlockSpec((1, D_MODEL), lambda i: (0, 0))],
        out_specs=[
            pl.BlockSpec((tm, D_MODEL), lambda i: (jnp.minimum(i, npb - 1), 0)),
            pl.BlockSpec((tm, D_MODEL), lambda i: (jnp.maximum(i - npb, 0), 0)),
        ],
        out_shape=[
            jax.ShapeDtypeStruct((n_prompt, D_MODEL), F32),
            jax.ShapeDtypeStruct((t - n_prompt, D_MODEL), F32),
        ],
        compiler_params=_params(("arbitrary",)),
    )(x, y0, y1, rf, modr, final_g.reshape(1, D_MODEL))


def _trunk(x_prompt, x_sample, c_all, lay, ada_w, ada_b, norm1_g, w_in, conv_w, conv_b, conv_ln_g, conv_ln_b,
           attn_out_g, conv_out_g, w_out, norm2_g, rg_w, rg_b, re_w, re_b, e_gate, e_up, e_down,
           final_g, *, tm_in=1024, tm_out=512, ts_conv=512, tm_moe=512, tm_res=512):
    batch, seq, dec_batch, dec_seq = lay
    ns = c_all.shape[0]
    ns8 = -(-ns // 8) * 8
    c_pad = jnp.zeros((ns8, D_MODEL), F32).at[:ns].set(c_all)
    mod = _ada_mod(c_pad, ada_w, ada_b)
    modrs = [mod[l, :ns].reshape(ns * 6, 1, D_MODEL) for l in range(DEPTH)]
    bias = _attn_bias()
    n_route = N_GROUPS + N_EXPERTS
    y_prompt = y_sample = None
    x, h = _prologue(x_prompt, x_sample, modrs[0], norm1_g[0], lay, tm_res)
    for l in range(DEPTH):
        modr = modrs[l]
        qkv, u = _proj_in(h, w_in[l].astype(BF16), tm_in)
        attn_p = _attention(qkv, bias, lay, group=0)
        attn_s = _attention(qkv, bias, lay, group=1)
        convn = _conv(u, conv_w[l], conv_b[l], conv_ln_g[l], conv_ln_b[l], conv_out_g[l], lay, ts_conv)
        wr = jnp.zeros((D_MODEL, ROUTE_LANES), F32).at[:, :n_route].set(
            jnp.concatenate([rg_w[l], re_w[l]], axis=1)).astype(BF16)
        br = jnp.zeros((1, ROUTE_LANES), F32).at[0, :n_route].set(jnp.concatenate([rg_b[l], re_b[l]]))
        x_mid, h2, ri, rf, cnt = _proj_out(attn_p, attn_s, convn, x, modr, attn_out_g[l], w_out[l].astype(BF16),
                                      norm2_g[l], wr, br, lay, tm_out)
        counts = cnt[0, N_GROUPS:N_GROUPS + N_EXPERTS].astype(jnp.int32)
        row_tok, pos, block_expert, n_valid, run_info = _dispatch(ri[:, :TOP_K], counts, tm_moe)
        xs = h2.at[row_tok].get(mode='promise_in_bounds')
        act = _gate_up(xs, e_gate, e_up, block_expert, n_valid, run_info, l, tm_moe)
        ys = _down(act, e_down, block_expert, n_valid, run_info, l, tm_moe)
        y0 = ys.at[pos[:, 0]].get(mode='promise_in_bounds')
        y1 = ys.at[pos[:, 1]].get(mode='promise_in_bounds')
        if l < DEPTH - 1:
            x, h = _combine(x_mid, y0, y1, rf, modr, modrs[l + 1], norm1_g[l + 1], lay, tm_res)
        else:
            y_prompt, y_sample = _final(x_mid, y0, y1, rf, modr, final_g, lay, tm_res)
    return (y_prompt.reshape(batch, seq, D_MODEL), y_sample.reshape(dec_batch, dec_seq, D_MODEL))


def kernel(x_prompt, x_sample, c_prompt, c_sample, ada_w, ada_b, norm1_g, w_in, conv_w, conv_b, conv_ln_g, conv_ln_b, attn_out_g, conv_out_g, w_out, norm2_g, rg_w, rg_b, re_w, re_b, e_gate, e_up, e_down, final_g):
    batch, seq, _ = x_prompt.shape
    dec_batch, dec_seq, _ = x_sample.shape
    lay = (batch, seq, dec_batch, dec_seq)
    c_all = jnp.concatenate([c_prompt, c_sample], axis=0)
    return _trunk(x_prompt.reshape(batch * seq, D_MODEL), x_sample.reshape(dec_batch * dec_seq, D_MODEL),
                  c_all, lay, ada_w, ada_b, norm1_g, w_in, conv_w, conv_b, conv_ln_g, conv_ln_b,
                  attn_out_g, conv_out_g, w_out, norm2_g, rg_w, rg_b, re_w, re_b,
                  e_gate, e_up, e_down, final_g)
```
